```python
import math
import jax, jax.numpy as jnp
from jax import lax
import numpy as np

D_MODEL = 1024
BATCH = 2
SEQ = 16384
DEPTH = 2
DEC_BATCH = 2
DEC_SEQ = 8192
PAST_LEN = 128

N_META = 16
GRID_W = 64
QUERY_BLOCK = 128
ROPE_THETA = 10000.0
NORM_EPS = 1e-6
A_HEADS = 8
A_KV_HEADS = 2
A_HEAD_DIM = 64
B_HEADS = 4
B_HEAD_DIM = 64
B_V_DIM = 2 * B_HEAD_DIM
C_HEADS = 4
C_NOPE = 128
C_ROPE = 64
C_V = 128
C_Q_LORA = 256
C_KV_LORA = 128
N_BRANCH = 3
MIX_W = 512
D_FF = 4 * D_MODEL
IN_SPLITS = (
    A_HEADS * A_HEAD_DIM, A_KV_HEADS * A_HEAD_DIM, A_KV_HEADS * A_HEAD_DIM,
    2 * B_HEADS * B_HEAD_DIM, 2 * B_HEADS * B_HEAD_DIM, B_HEADS * B_V_DIM,
    C_Q_LORA, C_KV_LORA, C_ROPE,
    N_BRANCH * D_MODEL,
)
IN_COLS = sum(IN_SPLITS)

kernel_name = 'hybrid_gqa_diff_mla_encoder'

F32 = jnp.float32


def rms_norm(x, g):
    xf = x.astype(F32)
    y = xf * lax.rsqrt(jnp.mean(xf * xf, axis=-1, keepdims=True) + NORM_EPS)
    return (y * g.astype(F32)).astype(x.dtype)


def rope(x, pos):
    d = x.shape[-1]
    half = d // 2
    inv = ROPE_THETA ** (-2.0 * jnp.arange(half, dtype=F32) / d)
    ang = pos[:, None] * inv[None, :]
    cos = jnp.cos(ang)[:, None, :]
    sin = jnp.sin(ang)[:, None, :]
    xf = x.astype(F32)
    x1, x2 = xf[..., :half], xf[..., half:]
    return jnp.concatenate([x1 * cos - x2 * sin, x2 * cos + x1 * sin], axis=-1).astype(x.dtype)


def axial_rope(x, row, col):
    h = x.shape[-1] // 2
    return jnp.concatenate([rope(x[..., :h], row), rope(x[..., h:], col)], axis=-1)


def token_positions(n_tokens):
    rows = n_tokens // GRID_W
    row = jnp.concatenate([jnp.full((N_META,), -1.0, F32),
                           jnp.repeat(jnp.arange(rows, dtype=F32), GRID_W)])
    col = jnp.concatenate([jnp.arange(N_META, dtype=F32),
                           jnp.tile(jnp.arange(GRID_W, dtype=F32), rows)])
    lin = jnp.arange(N_META + n_tokens, dtype=F32)
    return row, col, lin


def sweep_query_blocks(block_fn, q_args):
    head = block_fn(*[a[:, :N_META] for a in q_args])

    def split(a):
        rest = a[:, N_META:]
        nblk = rest.shape[1] // QUERY_BLOCK
        rest = rest.reshape(rest.shape[0], nblk, QUERY_BLOCK, *rest.shape[2:])
        return jnp.moveaxis(rest, 1, 0)

    body = lax.map(lambda args: block_fn(*args), tuple(split(a) for a in q_args))
    body = jnp.moveaxis(body, 0, 1)
    body = body.reshape(body.shape[0], -1, *body.shape[3:])
    return jnp.concatenate([head, body], axis=1)


def mixer_gqa(q, k, v, row, col, g_q, g_k):
    bsz, n, _ = q.shape
    grp = A_HEADS // A_KV_HEADS
    q = axial_rope(rms_norm(q.reshape(bsz, n, A_HEADS, A_HEAD_DIM), g_q), row, col)
    k = axial_rope(rms_norm(k.reshape(bsz, n, A_KV_HEADS, A_HEAD_DIM), g_k), row, col)
    v = v.reshape(bsz, n, A_KV_HEADS, A_HEAD_DIM)
    scale = A_HEAD_DIM ** -0.5

    def block(qb):
        qg = qb.reshape(qb.shape[0], qb.shape[1], A_KV_HEADS, grp, A_HEAD_DIM)
        s = jnp.einsum('bqhgd,bkhd->bhgqk', qg, k, preferred_element_type=F32) * scale
        p = jax.nn.softmax(s, axis=-1).astype(v.dtype)
        o = jnp.einsum('bhgqk,bkhd->bqhgd', p, v)
        return o.reshape(qb.shape[0], qb.shape[1], A_HEADS, A_HEAD_DIM)

    o = sweep_query_blocks(block, (q,))
    return o.reshape(bsz, n, A_HEADS * A_HEAD_DIM)


def mixer_diff(q, k, v, lin, g_q, g_k, lq1, lk1, lq2, lk2, g_sub, lam_init):
    bsz, n, _ = q.shape
    q = rms_norm(q.reshape(bsz, n, B_HEADS, 2, B_HEAD_DIM), g_q)
    k = rms_norm(k.reshape(bsz, n, B_HEADS, 2, B_HEAD_DIM), g_k)
    q1, q2 = q[..., 0, :], q[..., 1, :]
    k1, k2 = k[..., 0, :], k[..., 1, :]
    v = v.reshape(bsz, n, B_HEADS, B_V_DIM)
    lam = (jnp.exp(jnp.sum(lq1.astype(F32) * lk1.astype(F32)))
           - jnp.exp(jnp.sum(lq2.astype(F32) * lk2.astype(F32))) + lam_init)
    slopes = 2.0 ** (-8.0 * jnp.arange(1, B_HEADS + 1, dtype=F32) / B_HEADS)
    scale = B_HEAD_DIM ** -0.5

    def block(q1b, q2b, qpos):
        dist = jnp.abs(qpos[0][:, None] - lin[None, :])
        bias = -slopes[:, None, None] * dist[None]
        s1 = jnp.einsum('bqhd,bkhd->bhqk', q1b, k1, preferred_element_type=F32) * scale + bias
        s2 = jnp.einsum('bqhd,bkhd->bhqk', q2b, k2, preferred_element_type=F32) * scale + bias
        pdiff = jax.nn.softmax(s1, axis=-1) - lam * jax.nn.softmax(s2, axis=-1)
        return jnp.einsum('bhqk,bkhe->bqhe', pdiff.astype(v.dtype), v)

    o = sweep_query_blocks(block, (q1, q2, lin[None]))
    o = rms_norm(o, g_sub) * (1.0 - lam_init)
    return o.reshape(bsz, n, B_HEADS * B_V_DIM)


def mixer_mla(cq, ckv, kpe, lin, g_qa, w_qb, g_kva, w_kvb, g_q, g_k):
    bsz, n, _ = cq.shape
    q = (rms_norm(cq, g_qa) @ w_qb).reshape(bsz, n, C_HEADS, C_NOPE + C_ROPE)
    kv = (rms_norm(ckv, g_kva) @ w_kvb).reshape(bsz, n, C_HEADS, C_NOPE + C_V)
    k_nope, v = kv[..., :C_NOPE], kv[..., C_NOPE:]
    kpe = jnp.broadcast_to(kpe[:, :, None, :], (bsz, n, C_HEADS, C_ROPE))
    k = jnp.concatenate([k_nope, kpe], axis=-1)
    q = rms_norm(q, g_q)
    k = rms_norm(k, g_k)
    q = jnp.concatenate([q[..., :C_NOPE], rope(q[..., C_NOPE:], lin)], axis=-1)
    k = jnp.concatenate([k[..., :C_NOPE], rope(k[..., C_NOPE:], lin)], axis=-1)
    scale = (C_NOPE + C_ROPE) ** -0.5

    def block(qb):
        s = jnp.einsum('bqhd,bkhd->bhqk', qb, k, preferred_element_type=F32) * scale
        p = jax.nn.softmax(s, axis=-1).astype(v.dtype)
        return jnp.einsum('bhqk,bkhe->bqhe', p, v)

    o = sweep_query_blocks(block, (q,))
    return o.reshape(bsz, n, C_HEADS * C_V)


def layer(x, pos, l, p):
    row, col, lin = pos
    bsz, n, _ = x.shape
    h = rms_norm(x, p['attn_norm_g'][l])
    z = h @ p['w_in'][l]
    cuts = [int(c) for c in np.cumsum(IN_SPLITS)[:-1]]
    qa, ka, va, qb, kb, vb, cq, ckv, ckpe, gates = jnp.split(z, cuts, axis=-1)
    lam_init = 0.8 - 0.6 * math.exp(-0.3 * l)

    oa = mixer_gqa(qa, ka, va, row, col, p['a_q_norm_g'][l], p['a_k_norm_g'][l])
    ob = mixer_diff(qb, kb, vb, lin, p['b_q_norm_g'][l], p['b_k_norm_g'][l],
                    p['b_lambda_q1'][l], p['b_lambda_k1'][l], p['b_lambda_q2'][l],
                    p['b_lambda_k2'][l], p['b_subln_g'][l], lam_init)
    oc = mixer_mla(cq, ckv, ckpe, lin, p['c_q_a_norm_g'][l], p['c_w_q_b'][l],
                   p['c_kv_a_norm_g'][l], p['c_w_kv_b'][l], p['c_q_norm_g'][l], p['c_k_norm_g'][l])

    g = jax.nn.sigmoid((gates + p['b_gate'][l]).astype(F32)).astype(x.dtype)
    g = g.reshape(bsz, n, N_BRANCH, D_MODEL)
    merged = (g[:, :, 0] * (oa @ p['w_branch_a'][l])
              + g[:, :, 1] * (ob @ p['w_branch_b'][l])
              + g[:, :, 2] * (oc @ p['w_branch_c'][l]))
    x = x + merged @ p['w_out'][l]

    h2 = rms_norm(x, p['mlp_norm_g'][l])
    x = x + jnp.square(jax.nn.relu(h2 @ p['w_up'][l])) @ p['w_down'][l]
    return x


def trunk(x, meta_tokens, p):
    bsz, n_tokens, _ = x.shape
    pos = token_positions(n_tokens)
    meta = jnp.broadcast_to(meta_tokens.astype(x.dtype)[None], (bsz, N_META, D_MODEL))
    h = jnp.concatenate([meta, x], axis=1)
    for l in range(DEPTH):
        h = layer(h, pos, l, p)
    return h[:, N_META:]


def setup_inputs(seed: int = 0) -> dict:
    key = jax.random.key(seed)
    ks = jax.random.split(key, 32)

    def nrm(k, shape, scale):
        return scale * jax.random.normal(k, shape, F32)

    def gain(k, shape):
        return 1.0 + 0.02 * jax.random.normal(k, shape, F32)

    L = DEPTH
    return {
        'x_prompt': nrm(ks[0], (BATCH, SEQ, D_MODEL), 1.0),
        'x_sample': nrm(ks[1], (DEC_BATCH, DEC_SEQ, D_MODEL), 1.0),
        'meta_tokens': nrm(ks[2], (N_META, D_MODEL), 1.0),
        'attn_norm_g': gain(ks[3], (L, D_MODEL)),
        'w_in': nrm(ks[4], (L, D_MODEL, IN_COLS), D_MODEL ** -0.5),
        'b_gate': nrm(ks[5], (L, N_BRANCH * D_MODEL), 0.02),
        'a_q_norm_g': gain(ks[6], (L, A_HEAD_DIM)),
        'a_k_norm_g': gain(ks[7], (L, A_HEAD_DIM)),
        'b_q_norm_g': gain(ks[8], (L, B_HEAD_DIM)),
        'b_k_norm_g': gain(ks[9], (L, B_HEAD_DIM)),
        'b_lambda_q1': nrm(ks[10], (L, B_HEAD_DIM), 0.1),
        'b_lambda_k1': nrm(ks[11], (L, B_HEAD_DIM), 0.1),
        'b_lambda_q2': nrm(ks[12], (L, B_HEAD_DIM), 0.1),
        'b_lambda_k2': nrm(ks[13], (L, B_HEAD_DIM), 0.1),
        'b_subln_g': gain(ks[14], (L, B_V_DIM)),
        'c_q_a_norm_g': gain(ks[15], (L, C_Q_LORA)),
        'c_w_q_b': nrm(ks[16], (L, C_Q_LORA, C_HEADS * (C_NOPE + C_ROPE)), C_Q_LORA ** -0.5),
        'c_kv_a_norm_g': gain(ks[17], (L, C_KV_LORA)),
        'c_w_kv_b': nrm(ks[18], (L, C_KV_LORA, C_HEADS * (C_NOPE + C_V)), C_KV_LORA ** -0.5),
        'c_q_norm_g': gain(ks[19], (L, C_NOPE + C_ROPE)),
        'c_k_norm_g': gain(ks[20], (L, C_NOPE + C_ROPE)),
        'w_branch_a': nrm(ks[21], (L, A_HEADS * A_HEAD_DIM, D_MODEL), (A_HEADS * A_HEAD_DIM) ** -0.5),
        'w_branch_b': nrm(ks[22], (L, B_HEADS * B_V_DIM, D_MODEL), (B_HEADS * B_V_DIM) ** -0.5),
        'w_branch_c': nrm(ks[23], (L, C_HEADS * C_V, D_MODEL), (C_HEADS * C_V) ** -0.5),
        'w_out': nrm(ks[24], (L, D_MODEL, D_MODEL), D_MODEL ** -0.5),
        'mlp_norm_g': gain(ks[25], (L, D_MODEL)),
        'w_up': nrm(ks[26], (L, D_MODEL, D_FF), D_MODEL ** -0.5),
        'w_down': nrm(ks[27], (L, D_FF, D_MODEL), D_FF ** -0.5),
    }


def reference(x_prompt, x_sample, meta_tokens, attn_norm_g, w_in, b_gate,
              a_q_norm_g, a_k_norm_g, b_q_norm_g, b_k_norm_g,
              b_lambda_q1, b_lambda_k1, b_lambda_q2, b_lambda_k2, b_subln_g,
              c_q_a_norm_g, c_w_q_b, c_kv_a_norm_g, c_w_kv_b, c_q_norm_g, c_k_norm_g,
              w_branch_a, w_branch_b, w_branch_c, w_out, mlp_norm_g, w_up, w_down):
    p = dict(attn_norm_g=attn_norm_g, w_in=w_in, b_gate=b_gate,
             a_q_norm_g=a_q_norm_g, a_k_norm_g=a_k_norm_g,
             b_q_norm_g=b_q_norm_g, b_k_norm_g=b_k_norm_g,
             b_lambda_q1=b_lambda_q1, b_lambda_k1=b_lambda_k1,
             b_lambda_q2=b_lambda_q2, b_lambda_k2=b_lambda_k2, b_subln_g=b_subln_g,
             c_q_a_norm_g=c_q_a_norm_g, c_w_q_b=c_w_q_b,
             c_kv_a_norm_g=c_kv_a_norm_g, c_w_kv_b=c_w_kv_b,
             c_q_norm_g=c_q_norm_g, c_k_norm_g=c_k_norm_g,
             w_branch_a=w_branch_a, w_branch_b=w_branch_b, w_branch_c=w_branch_c,
             w_out=w_out, mlp_norm_g=mlp_norm_g, w_up=w_up, w_down=w_down)
    y_prompt = trunk(x_prompt, meta_tokens, p)
    y_sample = trunk(x_sample, meta_tokens, p)
    return (y_prompt, y_sample)
```

```python
import functools
import math

import numpy as np
import jax
import jax.numpy as jnp
from jax import lax
from jax.experimental import pallas as pl
from jax.experimental.pallas import tpu as pltpu

F32 = jnp.float32
BF16 = jnp.bfloat16

D_MODEL = 1024
DEPTH = 2
N_META = 16
GRID_W = 64
ROPE_THETA = 10000.0
NORM_EPS = 1e-6
A_HEADS, A_KV_HEADS, A_HEAD_DIM = 8, 2, 64
B_HEADS, B_HEAD_DIM = 4, 64
B_V_DIM = 2 * B_HEAD_DIM
C_HEADS, C_NOPE, C_ROPE, C_V = 4, 128, 64, 128
C_Q_LORA, C_KV_LORA = 256, 128
N_BRANCH = 3
D_FF = 4 * D_MODEL
IN_SPLITS = (
    A_HEADS * A_HEAD_DIM, A_KV_HEADS * A_HEAD_DIM, A_KV_HEADS * A_HEAD_DIM,
    2 * B_HEADS * B_HEAD_DIM, 2 * B_HEADS * B_HEAD_DIM, B_HEADS * B_V_DIM,
    C_Q_LORA, C_KV_LORA, C_ROPE,
    N_BRANCH * D_MODEL,
)
IN_COLS = sum(IN_SPLITS)

LOG2E = math.log2(math.e)
TQ = 256
TK = 512
TAIL = 256
ROW_TILE = 512
ONES_ROWS = 16
MASK_SCORE = -30000.0
VMEM_LIMIT = 52 * 1024 * 1024


def _cparams(sem):
    return pltpu.CompilerParams(dimension_semantics=sem, vmem_limit_bytes=VMEM_LIMIT)


def _norm_matmul_kernel(x_ref, g_ref, w_ref, o_ref):
    x = x_ref[...]
    r = lax.rsqrt(jnp.mean(x * x, axis=-1, keepdims=True) + NORM_EPS)
    h = (x * r * g_ref[...]).astype(BF16)
    o_ref[...] = jnp.dot(h, w_ref[...], preferred_element_type=F32)


def norm_matmul(x, g, w, tn):
    R, K = x.shape
    C = w.shape[1]
    return pl.pallas_call(
        _norm_matmul_kernel,
        out_shape=jax.ShapeDtypeStruct((R, C), F32),
        grid=(R // ROW_TILE, C // tn),
        in_specs=[
            pl.BlockSpec((ROW_TILE, K), lambda i, j: (i, 0)),
            pl.BlockSpec((1, K), lambda i, j: (0, 0)),
            pl.BlockSpec((K, tn), lambda i, j: (0, j)),
        ],
        out_specs=pl.BlockSpec((ROW_TILE, tn), lambda i, j: (i, j)),
        compiler_params=_cparams(("parallel", "arbitrary")),
        name="norm_matmul",
    )(x, g.reshape(1, K), w)


def _merge_kernel(oa_ref, ob_ref, oc_ref, gate_ref, bg_ref, x_ref,
                  wa_ref, wb_ref, wc_ref, wo_ref, out_ref):
    g = jax.nn.sigmoid(gate_ref[...] + bg_ref[...])
    pa = jnp.dot(oa_ref[...], wa_ref[...], preferred_element_type=F32)
    pb = jnp.dot(ob_ref[...], wb_ref[...], preferred_element_type=F32)
    pc = jnp.dot(oc_ref[...], wc_ref[...], preferred_element_type=F32)
    merged = (g[:, :D_MODEL] * pa + g[:, D_MODEL:2 * D_MODEL] * pb
              + g[:, 2 * D_MODEL:] * pc)
    out_ref[...] = x_ref[...] + jnp.dot(merged.astype(BF16), wo_ref[...],
                                        preferred_element_type=F32)


def merge_branches(oa, ob, oc, gates, b_gate, x, wa, wb, wc, wo):
    R = x.shape[0]
    mix = oa.shape[1]
    row = lambda c: pl.BlockSpec((ROW_TILE, c), lambda i: (i, 0))
    full = lambda a: pl.BlockSpec(a.shape, lambda i: (0, 0))
    return pl.pallas_call(
        _merge_kernel,
        out_shape=jax.ShapeDtypeStruct((R, D_MODEL), F32),
        grid=(R // ROW_TILE,),
        in_specs=[row(mix), row(mix), row(mix), row(N_BRANCH * D_MODEL),
                  pl.BlockSpec((1, N_BRANCH * D_MODEL), lambda i: (0, 0)),
                  row(D_MODEL), full(wa), full(wb), full(wc), full(wo)],
        out_specs=row(D_MODEL),
        compiler_params=_cparams(("parallel",)),
        name="merge_branches",
    )(oa, ob, oc, gates, b_gate.reshape(1, -1), x, wa, wb, wc, wo)


def _mlp_kernel(x_ref, g_ref, wu_ref, wd_ref, out_ref, *, ff_chunk):
    x = x_ref[...]
    r = lax.rsqrt(jnp.mean(x * x, axis=-1, keepdims=True) + NORM_EPS)
    h = (x * r * g_ref[...]).astype(BF16)
    acc = x
    for c in range(D_FF // ff_chunk):
        u = jnp.dot(h, wu_ref[:, c * ff_chunk:(c + 1) * ff_chunk],
                    preferred_element_type=F32)
        a = jnp.square(jnp.maximum(u, 0.0)).astype(BF16)
        acc = acc + jnp.dot(a, wd_ref[c * ff_chunk:(c + 1) * ff_chunk, :],
                            preferred_element_type=F32)
    out_ref[...] = acc


def mlp_block(x, g, wu, wd):
    R = x.shape[0]
    return pl.pallas_call(
        functools.partial(_mlp_kernel, ff_chunk=1024),
        out_shape=jax.ShapeDtypeStruct((R, D_MODEL), F32),
        grid=(R // ROW_TILE,),
        in_specs=[
            pl.BlockSpec((ROW_TILE, D_MODEL), lambda i: (i, 0)),
            pl.BlockSpec((1, D_MODEL), lambda i: (0, 0)),
            pl.BlockSpec(wu.shape, lambda i: (0, 0)),
            pl.BlockSpec(wd.shape, lambda i: (0, 0)),
        ],
        out_specs=pl.BlockSpec((ROW_TILE, D_MODEL), lambda i: (i, 0)),
        compiler_params=_cparams(("parallel",)),
        name="mlp_block",
    )(x, g.reshape(1, -1), wu, wd)


def _dot_nt(a, b):
    return lax.dot_general(a, b, (((1,), (1,)), ((), ())), preferred_element_type=F32)


def _attn_kernel(*refs, n_chunks, dv, feat0, alibi):
    if alibi:
        sigma_ref, q_ref, kx_ref, vt_ref, o_ref, m_ref, acc_ref = refs
    else:
        q_ref, kx_ref, vt_ref, o_ref, m_ref, acc_ref = refs
    tq = q_ref.shape[0]
    n_main = n_chunks * TK
    qi = pl.program_id(2)

    m_ref[...] = jnp.full(m_ref.shape, -1e30, F32)
    acc_ref[...] = jnp.zeros(acc_ref.shape, F32)

    qa = q_ref[...]
    ex = jnp.exp if alibi else jnp.exp2

    def update(s, shift, vt):
        m_old = m_ref[...]
        m_new = jnp.maximum(m_old, jnp.max(s, axis=0, keepdims=True) + shift)
        alpha = ex(m_old - m_new)
        p = ex(s - (m_new - shift)).astype(BF16)
        acc_ref[...] = acc_ref[...] * alpha + jnp.dot(vt, p, preferred_element_type=F32)
        m_ref[...] = m_new

    def main_chunk(j):
        start = pl.multiple_of(j * TK, TK)
        return kx_ref[pl.ds(start, TK), :], vt_ref[:, pl.ds(start, TK)]

    def tail_chunk():
        return kx_ref[n_main:n_main + TAIL, :], vt_ref[:, n_main:n_main + TAIL]

    if not alibi:
        def body(j, carry):
            kx, vt = main_chunk(j)
            update(_dot_nt(kx, qa), 0.0, vt)
            return carry
        lax.fori_loop(0, n_chunks, body, 0)
        kx, vt = tail_chunk()
        update(_dot_nt(kx, qa), 0.0, vt)
    else:
        lane = lax.broadcasted_iota(jnp.int32, qa.shape, 1)
        flip = (lane >= feat0) & (lane < feat0 + 4)
        qb = jnp.where(flip, -qa, qa)
        sigma = sigma_ref[pl.program_id(1)]
        n_q_main = n_main // tq
        is_meta = qi == n_q_main
        base_i = jnp.where(is_meta, 0, N_META + qi * tq)
        j_before = jnp.where(is_meta, 0, (qi * tq) // TK)
        j_after = jnp.where(is_meta, 0, ((qi + 1) * tq + TK - 1) // TK)

        def delta(j):
            return (base_i - (N_META + j * TK)).astype(F32)

        def before_body(j, carry):
            kx, vt = main_chunk(j)
            update(_dot_nt(kx, qa), -sigma * delta(j), vt)
            return carry

        def diag_body(j, carry):
            kx, vt = main_chunk(j)
            c = sigma * delta(j)
            s = jnp.minimum(_dot_nt(kx, qa) - c, _dot_nt(kx, qb) + c)
            update(s, 0.0, vt)
            return carry

        def after_body(j, carry):
            kx, vt = main_chunk(j)
            update(_dot_nt(kx, qb), sigma * delta(j), vt)
            return carry

        lax.fori_loop(0, j_before, before_body, 0)
        lax.fori_loop(j_before, j_after, diag_body, 0)
        lax.fori_loop(j_after, n_chunks, after_body, 0)

        @pl.when(is_meta)
        def _():
            kx, vt = tail_chunk()
            update(jnp.minimum(_dot_nt(kx, qa), _dot_nt(kx, qb)), 0.0, vt)

        @pl.when(jnp.logical_not(is_meta))
        def _():
            kx, vt = tail_chunk()
            update(_dot_nt(kx, qa), -sigma * base_i.astype(F32), vt)

    acc = acc_ref[...]
    o_ref[...] = acc[:dv, :] / acc[dv:dv + 1, :]


def flash_attention(q, kx, vt, *, dv, feat0, sigma=None):
    B, H, N, Kd = q.shape
    G, Gv = kx.shape[1], vt.shape[1]
    dvx = vt.shape[2]
    n_chunks = (N - TAIL) // TK
    assert n_chunks * TK + TAIL == N and (N - TAIL) % TQ == 0
    alibi = sigma is not None
    in_specs = [
        pl.BlockSpec((None, None, TQ, Kd), lambda b, h, i: (b, h, i, 0)),
        pl.BlockSpec((None, None, N, Kd), lambda b, h, i: (b, h // (H // G), 0, 0)),
        pl.BlockSpec((None, None, dvx, N), lambda b, h, i: (b, h // (H // Gv), 0, 0)),
    ]
    args = [q, kx, vt]
    if alibi:
        in_specs = [pl.BlockSpec(memory_space=pltpu.SMEM)] + in_specs
        args = [sigma] + args
    return pl.pallas_call(
        functools.partial(_attn_kernel, n_chunks=n_chunks, dv=dv, feat0=feat0, alibi=alibi),
        out_shape=jax.ShapeDtypeStruct((B, H, dv, N), F32),
        grid=(B, H, N // TQ),
        in_specs=in_specs,
        out_specs=pl.BlockSpec((None, None, dv, TQ), lambda b, h, i: (b, h, 0, i)),
        scratch_shapes=[pltpu.VMEM((1, TQ), F32), pltpu.VMEM((dvx, TQ), F32)],
        compiler_params=_cparams(("parallel", "parallel", "arbitrary")),
        name="flash_alibi" if alibi else "flash_plain",
    )(*args)


def _rms(x, g):
    y = x * lax.rsqrt(jnp.mean(x * x, axis=-1, keepdims=True) + NORM_EPS)
    return y * g


def _rope(x, pos):
    d = x.shape[-1]
    half = d // 2
    inv = ROPE_THETA ** (-2.0 * jnp.arange(half, dtype=F32) / d)
    ang = pos[:, None] * inv[None, :]
    cos = jnp.cos(ang)[:, None, :]
    sin = jnp.sin(ang)[:, None, :]
    x1, x2 = x[..., :half], x[..., half:]
    return jnp.concatenate([x1 * cos - x2 * sin, x2 * cos + x1 * sin], axis=-1)


def _axial_rope(x, row, col):
    h = x.shape[-1] // 2
    return jnp.concatenate([_rope(x[..., :h], row), _rope(x[..., h:], col)], axis=-1)


def _positions(n):
    pad = TAIL - N_META
    rows = n // GRID_W
    row = np.concatenate([np.repeat(np.arange(rows), GRID_W), np.full(N_META, -1.0), np.zeros(pad)])
    col = np.concatenate([np.tile(np.arange(GRID_W), rows), np.arange(N_META), np.zeros(pad)])
    lin = np.concatenate([N_META + np.arange(n), np.arange(N_META), np.zeros(pad)])
    valid = np.concatenate([np.ones(n + N_META), np.zeros(pad)])
    f = lambda a: jnp.asarray(a, F32)
    return f(row), f(col), f(lin), f(valid)


def _heads_first(x):
    return jnp.transpose(x, (0, 2, 1, 3))


def _pad_last(x, width):
    return jnp.pad(x, [(0, 0)] * (x.ndim - 1) + [(0, width - x.shape[-1])])


def _kx(k, valid, width):
    flag = jnp.broadcast_to((1.0 - valid)[None, :, None, None], k.shape[:3] + (1,))
    k = k * valid[None, :, None, None]
    return _heads_first(_pad_last(jnp.concatenate([k, flag], -1), width)).astype(BF16)


def _qx(q, width):
    mask = jnp.full(q.shape[:3] + (1,), MASK_SCORE, F32)
    return _heads_first(_pad_last(jnp.concatenate([q, mask], -1), width)).astype(BF16)


def _vt(v, valid):
    ones = jnp.ones(v.shape[:3] + (1,), F32)
    v = _pad_last(jnp.concatenate([v, ones], -1), v.shape[-1] + ONES_ROWS)
    v = v * valid[None, :, None, None]
    return jnp.transpose(v, (0, 2, 3, 1)).astype(BF16)


def _out_tokens(o):
    return jnp.transpose(o, (0, 3, 1, 2))


def _mixer_gqa(q, k, v, pos, g_q, g_k):
    row, col, _, valid = pos
    B, N, _ = q.shape
    q = _axial_rope(_rms(q.reshape(B, N, A_HEADS, A_HEAD_DIM), g_q), row, col)
    k = _axial_rope(_rms(k.reshape(B, N, A_KV_HEADS, A_HEAD_DIM), g_k), row, col)
    v = v.reshape(B, N, A_KV_HEADS, A_HEAD_DIM)
    q = q * (A_HEAD_DIM ** -0.5 * LOG2E)
    o = flash_attention(_qx(q, 128), _kx(k, valid, 128), _vt(v, valid),
                        dv=A_HEAD_DIM, feat0=A_HEAD_DIM)
    return _out_tokens(o).reshape(B, N, A_HEADS * A_HEAD_DIM)


def _mixer_diff(q, k, v, pos, g_q, g_k, lq1, lk1, lq2, lk2, g_sub, lam_init):
    _, _, lin, valid = pos
    B, N, _ = q.shape
    q = _rms(q.reshape(B, N, 2 * B_HEADS, B_HEAD_DIM), g_q) * (B_HEAD_DIM ** -0.5)
    k = _rms(k.reshape(B, N, 2 * B_HEADS, B_HEAD_DIM), g_k)
    v = v.reshape(B, N, B_HEADS, B_V_DIM)
    lam = (jnp.exp(jnp.sum(lq1 * lk1)) - jnp.exp(jnp.sum(lq2 * lk2)) + lam_init)
    slopes = 2.0 ** (-8.0 * jnp.arange(1, B_HEADS + 1, dtype=F32) / B_HEADS)
    sig = jnp.repeat(slopes, 2)
    idx = np.arange(N)
    ii = jnp.asarray(idx % TQ, F32)
    jj = idx % TK
    one = jnp.ones((N,), F32)
    kfeat = jnp.stack([one, one, jnp.asarray(jj // 256, F32), jnp.asarray(jj % 256, F32)], -1)
    ii_hi = jnp.floor(ii / 256.0) * 256.0
    qfeat = jnp.stack([-(ii_hi)[:, None] * sig[None, :], -(ii - ii_hi)[:, None] * sig[None, :],
                       256.0 * jnp.broadcast_to(sig[None, :], (N, 2 * B_HEADS)),
                       jnp.broadcast_to(sig[None, :], (N, 2 * B_HEADS))], -1)
    mask = jnp.full((B, N, 2 * B_HEADS, 1), MASK_SCORE, F32)
    qx = jnp.concatenate([q, jnp.broadcast_to(qfeat[None], (B,) + qfeat.shape), mask], -1)
    qx = _heads_first(_pad_last(qx, 128)).astype(BF16)
    flag = jnp.broadcast_to((1.0 - valid)[None, :, None, None], (B, N, 2 * B_HEADS, 1))
    kfe = jnp.broadcast_to(kfeat[None, :, None, :], (B, N, 2 * B_HEADS, 4))
    kx = jnp.concatenate([k, kfe, flag], -1) * jnp.concatenate(
        [jnp.broadcast_to(valid[None, :, None, None], (B, N, 2 * B_HEADS, B_HEAD_DIM + 4)),
         jnp.ones((B, N, 2 * B_HEADS, 1), F32)], -1)
    kx = _heads_first(_pad_last(kx, 128)).astype(BF16)
    o = flash_attention(qx, kx, _vt(v, valid), dv=B_V_DIM, feat0=B_HEAD_DIM, sigma=sig)
    o = _out_tokens(o).reshape(B, N, B_HEADS, 2, B_V_DIM)
    o = o[:, :, :, 0] - lam * o[:, :, :, 1]
    o = _rms(o, g_sub) * (1.0 - lam_init)
    return o.reshape(B, N, B_HEADS * B_V_DIM)


def _mixer_mla(cq, ckv, kpe, pos, g_qa, w_qb, g_kva, w_kvb, g_q, g_k):
    _, _, lin, valid = pos
    B, N, _ = cq.shape
    dqk = C_NOPE + C_ROPE
    q = norm_matmul(cq.reshape(B * N, C_Q_LORA), g_qa, w_qb.astype(BF16), tn=C_HEADS * dqk)
    kv = norm_matmul(ckv.reshape(B * N, C_KV_LORA), g_kva, w_kvb.astype(BF16),
                     tn=C_HEADS * (C_NOPE + C_V))
    q = q.reshape(B, N, C_HEADS, dqk)
    kv = kv.reshape(B, N, C_HEADS, C_NOPE + C_V)
    k_nope, v = kv[..., :C_NOPE], kv[..., C_NOPE:]
    kpe = jnp.broadcast_to(kpe[:, :, None, :], (B, N, C_HEADS, C_ROPE))
    k = jnp.concatenate([k_nope, kpe], axis=-1)
    q = _rms(q, g_q)
    k = _rms(k, g_k)
    q = jnp.concatenate([q[..., :C_NOPE], _rope(q[..., C_NOPE:], lin)], axis=-1)
    k = jnp.concatenate([k[..., :C_NOPE], _rope(k[..., C_NOPE:], lin)], axis=-1)
    q = q * (dqk ** -0.5 * LOG2E)
    o = flash_attention(_qx(q, 256), _kx(k, valid, 256), _vt(v, valid), dv=C_V, feat0=dqk)
    return _out_tokens(o).reshape(B, N, C_HEADS * C_V)


def _layer(x, pos, l, p):
    B, N, _ = x.shape
    R = B * N
    x2 = x.reshape(R, D_MODEL)
    w_in = _pad_last(p['w_in'][l], 6144).astype(BF16)
    z = norm_matmul(x2, p['attn_norm_g'][l], w_in, tn=1536).reshape(B, N, -1)
    cuts = [int(c) for c in np.cumsum(IN_SPLITS)]
    parts = [z[..., a:b] for a, b in zip([0] + cuts[:-1], cuts)]
    qa, ka, va, qb, kb, vb, cq, ckv, ckpe, gates = parts
    lam_init = 0.8 - 0.6 * math.exp(-0.3 * l)

    oa = _mixer_gqa(qa, ka, va, pos, p['a_q_norm_g'][l], p['a_k_norm_g'][l])
    ob = _mixer_diff(qb, kb, vb, pos, p['b_q_norm_g'][l], p['b_k_norm_g'][l],
                     p['b_lambda_q1'][l], p['b_lambda_k1'][l], p['b_lambda_q2'][l],
                     p['b_lambda_k2'][l], p['b_subln_g'][l], lam_init)
    oc = _mixer_mla(cq, ckv, ckpe, pos, p['c_q_a_norm_g'][l], p['c_w_q_b'][l],
                    p['c_kv_a_norm_g'][l], p['c_w_kv_b'][l], p['c_q_norm_g'][l],
                    p['c_k_norm_g'][l])

    flat = lambda o: o.reshape(R, -1).astype(BF16)
    x2 = merge_branches(flat(oa), flat(ob), flat(oc), gates.reshape(R, -1), p['b_gate'][l], x2,
                        p['w_branch_a'][l].astype(BF16), p['w_branch_b'][l].astype(BF16),
                        p['w_branch_c'][l].astype(BF16), p['w_out'][l].astype(BF16))
    x2 = mlp_block(x2, p['mlp_norm_g'][l], p['w_up'][l].astype(BF16), p['w_down'][l].astype(BF16))
    return x2.reshape(B, N, D_MODEL)


def _trunk(x, meta_tokens, p):
    B, n, _ = x.shape
    assert n % TK == 0 and n % TQ == 0 and n % GRID_W == 0
    pos = _positions(n)
    meta = jnp.broadcast_to(meta_tokens[None], (B, N_META, D_MODEL))
    h = jnp.concatenate([x, meta, jnp.zeros((B, TAIL - N_META, D_MODEL), F32)], axis=1)
    for l in range(DEPTH):
        h = _layer(h, pos, l, p)
    return h[:, :n]


def kernel(x_prompt, x_sample, meta_tokens, attn_norm_g, w_in, b_gate, a_q_norm_g, a_k_norm_g, b_q_norm_g, b_k_norm_g, b_lambda_q1, b_lambda_k1, b_lambda_q2, b_lambda_k2, b_subln_g, c_q_a_norm_g, c_w_q_b, c_kv_a_norm_g, c_w_kv_b, c_q_norm_g, c_k_norm_g, w_branch_a, w_branch_b, w_branch_c, w_out, mlp_norm_g, w_up, w_down):
    p = dict(attn_norm_g=attn_norm_g, w_in=w_in, b_gate=b_gate,
             a_q_norm_g=a_q_norm_g, a_k_norm_g=a_k_norm_g,
             b_q_norm_g=b_q_norm_g, b_k_norm_g=b_k_norm_g,
             b_lambda_q1=b_lambda_q1, b_lambda_k1=b_lambda_k1,
             b_lambda_q2=b_lambda_q2, b_lambda_k2=b_lambda_k2, b_subln_g=b_subln_g,
             c_q_a_norm_g=c_q_a_norm_g, c_w_q_b=c_w_q_b,
             c_kv_a_norm_g=c_kv_a_norm_g, c_w_kv_b=c_w_kv_b,
             c_q_norm_g=c_q_norm_g, c_k_norm_g=c_k_norm_g,
             w_branch_a=w_branch_a, w_branch_b=w_branch_b, w_branch_c=w_branch_c,
             w_out=w_out, mlp_norm_g=mlp_norm_g, w_up=w_up, w_down=w_down)
    return (_trunk(x_prompt, meta_tokens, p), _trunk(x_sample, meta_tokens, p))
```

```python
import functools
import math

import numpy as np
import jax
import jax.numpy as jnp
from jax import lax
from jax.experimental import pallas as pl
from jax.experimental.pallas import tpu as pltpu

F32 = jnp.float32
BF16 = jnp.bfloat16

D_MODEL = 1024
DEPTH = 2
N_META = 16
GRID_W = 64
ROPE_THETA = 10000.0
NORM_EPS = 1e-6
A_HEADS, A_KV_HEADS, A_HEAD_DIM = 8, 2, 64
B_HEADS, B_HEAD_DIM = 4, 64
B_V_DIM = 2 * B_HEAD_DIM
C_HEADS, C_NOPE, C_ROPE, C_V = 4, 128, 64, 128
C_Q_LORA, C_KV_LORA = 256, 128
N_BRANCH = 3
D_FF = 4 * D_MODEL
IN_SPLITS = (
    A_HEADS * A_HEAD_DIM, A_KV_HEADS * A_HEAD_DIM, A_KV_HEADS * A_HEAD_DIM,
    2 * B_HEADS * B_HEAD_DIM, 2 * B_HEADS * B_HEAD_DIM, B_HEADS * B_V_DIM,
    C_Q_LORA, C_KV_LORA, C_ROPE,
    N_BRANCH * D_MODEL,
)
IN_COLS = sum(IN_SPLITS)

LOG2E = math.log2(math.e)
SUB = 256
N_STREAMS = 2
TK = 512
TAIL = 256
ROW_TILE = 512
ONES_ROWS = 16
MASK_SCORE = -30000.0
VMEM_LIMIT = 52 * 1024 * 1024


def _cparams(sem):
    return pltpu.CompilerParams(dimension_semantics=sem, vmem_limit_bytes=VMEM_LIMIT)


def _norm_matmul_kernel(x_ref, g_ref, w_ref, o_ref):
    x = x_ref[...]
    r = lax.rsqrt(jnp.mean(x * x, axis=-1, keepdims=True) + NORM_EPS)
    h = (x * r * g_ref[...]).astype(BF16)
    o_ref[...] = jnp.dot(h, w_ref[...], preferred_element_type=F32)


def norm_matmul(x, g, w, tn):
    R, K = x.shape
    C = w.shape[1]
    return pl.pallas_call(
        _norm_matmul_kernel,
        out_shape=jax.ShapeDtypeStruct((R, C), F32),
        grid=(R // ROW_TILE, C // tn),
        in_specs=[
            pl.BlockSpec((ROW_TILE, K), lambda i, j: (i, 0)),
            pl.BlockSpec((1, K), lambda i, j: (0, 0)),
            pl.BlockSpec((K, tn), lambda i, j: (0, j)),
        ],
        out_specs=pl.BlockSpec((ROW_TILE, tn), lambda i, j: (i, j)),
        compiler_params=_cparams(("parallel", "arbitrary")),
        name="norm_matmul",
    )(x, g.reshape(1, K), w)


def _merge_kernel(oa_ref, ob_ref, oc_ref, gate_ref, bg_ref, x_ref,
                  wa_ref, wb_ref, wc_ref, wo_ref, out_ref):
    g = jax.nn.sigmoid(gate_ref[...] + bg_ref[...])
    pa = jnp.dot(oa_ref[...], wa_ref[...], preferred_element_type=F32)
    pb = jnp.dot(ob_ref[...], wb_ref[...], preferred_element_type=F32)
    pc = jnp.dot(oc_ref[...], wc_ref[...], preferred_element_type=F32)
    merged = (g[:, :D_MODEL] * pa + g[:, D_MODEL:2 * D_MODEL] * pb
              + g[:, 2 * D_MODEL:] * pc)
    out_ref[...] = x_ref[...] + jnp.dot(merged.astype(BF16), wo_ref[...],
                                        preferred_element_type=F32)


def merge_branches(oa, ob, oc, gates, b_gate, x, wa, wb, wc, wo):
    R = x.shape[0]
    mix = oa.shape[1]
    row = lambda c: pl.BlockSpec((ROW_TILE, c), lambda i: (i, 0))
    full = lambda a: pl.BlockSpec(a.shape, lambda i: (0, 0))
    return pl.pallas_call(
        _merge_kernel,
        out_shape=jax.ShapeDtypeStruct((R, D_MODEL), F32),
        grid=(R // ROW_TILE,),
        in_specs=[row(mix), row(mix), row(mix), row(N_BRANCH * D_MODEL),
                  pl.BlockSpec((1, N_BRANCH * D_MODEL), lambda i: (0, 0)),
                  row(D_MODEL), full(wa), full(wb), full(wc), full(wo)],
        out_specs=row(D_MODEL),
        compiler_params=_cparams(("parallel",)),
        name="merge_branches",
    )(oa, ob, oc, gates, b_gate.reshape(1, -1), x, wa, wb, wc, wo)


def _mlp_kernel(x_ref, g_ref, wu_ref, wd_ref, out_ref, *, ff_chunk):
    x = x_ref[...]
    r = lax.rsqrt(jnp.mean(x * x, axis=-1, keepdims=True) + NORM_EPS)
    h = (x * r * g_ref[...]).astype(BF16)
    acc = x
    for c in range(D_FF // ff_chunk):
        u = jnp.dot(h, wu_ref[:, c * ff_chunk:(c + 1) * ff_chunk],
                    preferred_element_type=F32)
        a = jnp.square(jnp.maximum(u, 0.0)).astype(BF16)
        acc = acc + jnp.dot(a, wd_ref[c * ff_chunk:(c + 1) * ff_chunk, :],
                            preferred_element_type=F32)
    out_ref[...] = acc


def mlp_block(x, g, wu, wd):
    R = x.shape[0]
    return pl.pallas_call(
        functools.partial(_mlp_kernel, ff_chunk=1024),
        out_shape=jax.ShapeDtypeStruct((R, D_MODEL), F32),
        grid=(R // ROW_TILE,),
        in_specs=[
            pl.BlockSpec((ROW_TILE, D_MODEL), lambda i: (i, 0)),
            pl.BlockSpec((1, D_MODEL), lambda i: (0, 0)),
            pl.BlockSpec(wu.shape, lambda i: (0, 0)),
            pl.BlockSpec(wd.shape, lambda i: (0, 0)),
        ],
        out_specs=pl.BlockSpec((ROW_TILE, D_MODEL), lambda i: (i, 0)),
        compiler_params=_cparams(("parallel",)),
        name="mlp_block",
    )(x, g.reshape(1, -1), wu, wd)


def _dot_nt(a, b):
    return lax.dot_general(a, b, (((1,), (1,)), ((), ())), preferred_element_type=F32)


def _attn_kernel(*refs, n_chunks, dv, feat0, alibi, meta_tile, n_streams):
    if alibi:
        sigma_ref, q_ref, kx_ref, vt_ref, o_ref = refs[:5]
    else:
        q_ref, kx_ref, vt_ref, o_ref = refs[:4]
    m_ref, acc_ref, s_ring, bm_ring, p_ring, al_ring = refs[-6:]
    n_main = n_chunks * TK
    tqt = n_streams * SUB
    qi = pl.program_id(2)
    streams = range(n_streams)

    m_ref[...] = jnp.full(m_ref.shape, -1e30, F32)
    acc_ref[...] = jnp.zeros(acc_ref.shape, F32)

    qa = [q_ref[s * SUB:(s + 1) * SUB, :] for s in streams]
    ex = jnp.exp if alibi else jnp.exp2

    def update(st, s, shift, vt):
        m_old = m_ref[st]
        m_new = jnp.maximum(m_old, jnp.max(s, axis=0, keepdims=True) + shift)
        alpha = ex(m_old - m_new)
        p = ex(s - (m_new - shift)).astype(BF16)
        acc_ref[st] = acc_ref[st] * alpha + jnp.dot(vt, p, preferred_element_type=F32)
        m_ref[st] = m_new

    def main_chunk(j):
        start = pl.multiple_of(j * TK, TK)
        return kx_ref[pl.ds(start, TK), :], vt_ref[:, pl.ds(start, TK)]

    def tail_chunk():
        return kx_ref[n_main:n_main + TAIL, :], vt_ref[:, n_main:n_main + TAIL]

    if alibi:
        lane = lax.broadcasted_iota(jnp.int32, qa[0].shape, 1)
        flip = (lane >= feat0) & (lane < feat0 + 4)
        qb = [jnp.where(flip, -q, q) for q in qa]
        sigma = sigma_ref[pl.program_id(1)]
        n_diag = 0 if meta_tile else max(1, tqt // TK)
        j_before = jnp.int32(0) if meta_tile else (qi * tqt) // TK
    else:
        n_diag = 0
    n_pure = n_chunks - n_diag
    assert n_pure >= 2

    def base_i(st):
        return jnp.int32(0) if meta_tile else N_META + qi * tqt + st * SUB

    def delta(st, b):
        return (base_i(st) - (N_META + b * TK)).astype(F32)

    def pure_chunk(t):
        if not alibi:
            return t, None
        after = t >= j_before
        return t + jnp.where(after, n_diag, 0), after

    def stage_a(t, slot):
        b, after = pure_chunk(t)
        kx = kx_ref[pl.ds(pl.multiple_of(b * TK, TK), TK), :]
        for st in streams:
            q = jnp.where(after, qb[st], qa[st]) if alibi else qa[st]
            s = _dot_nt(kx, q)
            s_ring[slot, st] = s
            bm_ring[slot, st] = jnp.max(s, axis=0, keepdims=True)

    def stage_b(t, slot):
        b, after = pure_chunk(t)
        for st in streams:
            shift = 0.0
            if alibi:
                d = sigma * delta(st, b)
                shift = jnp.where(after, d, -d)
            m_old = m_ref[st]
            m_new = jnp.maximum(m_old, bm_ring[slot, st] + shift)
            al_ring[slot, st] = ex(m_old - m_new)
            p_ring[slot, st] = ex(s_ring[slot, st] - (m_new - shift)).astype(BF16)
            m_ref[st] = m_new

    def stage_c(t, slot):
        b, _ = pure_chunk(t)
        vt = vt_ref[:, pl.ds(pl.multiple_of(b * TK, TK), TK)]
        for st in streams:
            acc_ref[st] = acc_ref[st] * al_ring[slot, st] + jnp.dot(
                vt, p_ring[slot, st], preferred_element_type=F32)

    stage_a(0, 0)
    stage_a(1, 1)
    stage_b(0, 0)
    n_pairs = (n_pure - 2) // 2

    def pair(tp, carry):
        t = 2 * tp
        stage_a(t + 2, 0)
        stage_b(t + 1, 1)
        stage_c(t, 0)
        stage_a(t + 3, 1)
        stage_b(t + 2, 0)
        stage_c(t + 1, 1)
        return carry

    lax.fori_loop(0, n_pairs, pair, 0)
    for a in range(2 + 2 * n_pairs, n_pure + 2):
        if a < n_pure:
            stage_a(a, a % 2)
        if a - 1 < n_pure:
            stage_b(a - 1, (a - 1) % 2)
        stage_c(a - 2, a % 2)

    kx_t, vt_t = tail_chunk()
    if not alibi:
        for st in streams:
            update(st, _dot_nt(kx_t, qa[st]), 0.0, vt_t)
    elif meta_tile:
        update(0, jnp.minimum(_dot_nt(kx_t, qa[0]), _dot_nt(kx_t, qb[0])), 0.0, vt_t)
    else:
        for d in range(n_diag):
            kx, vt = main_chunk(j_before + d)
            for st in streams:
                c = sigma * delta(st, j_before + d)
                s = jnp.minimum(_dot_nt(kx, qa[st]) - c, _dot_nt(kx, qb[st]) + c)
                update(st, s, 0.0, vt)
        for st in streams:
            update(st, _dot_nt(kx_t, qa[st]), -sigma * base_i(st).astype(F32), vt_t)

    for st in streams:
        acc = acc_ref[st]
        o_ref[:, st * SUB:(st + 1) * SUB] = acc[:dv, :] / acc[dv:dv + 1, :]


def _flash_call(q, kx, vt, sigma, *, dv, feat0, meta_tile, n_streams, n_q):
    B, H, _, Kd = q.shape
    G, Gv = kx.shape[1], vt.shape[1]
    N, dvx = kx.shape[2], vt.shape[2]
    n_chunks = (N - TAIL) // TK
    assert n_chunks * TK + TAIL == N
    tqt = n_streams * SUB
    assert n_q % tqt == 0
    alibi = sigma is not None
    in_specs = [
        pl.BlockSpec((None, None, tqt, Kd), lambda b, h, i: (b, h, i, 0)),
        pl.BlockSpec((None, None, N, Kd), lambda b, h, i: (b, h // (H // G), 0, 0)),
        pl.BlockSpec((None, None, dvx, N), lambda b, h, i: (b, h // (H // Gv), 0, 0)),
    ]
    args = [q, kx, vt]
    if alibi:
        in_specs = [pl.BlockSpec(memory_space=pltpu.SMEM)] + in_specs
        args = [sigma] + args
    return pl.pallas_call(
        functools.partial(_attn_kernel, n_chunks=n_chunks, dv=dv, feat0=feat0, alibi=alibi,
                          meta_tile=meta_tile, n_streams=n_streams),
        out_shape=jax.ShapeDtypeStruct((B, H, dv, n_q), F32),
        grid=(B, H, n_q // tqt),
        in_specs=in_specs,
        out_specs=pl.BlockSpec((None, None, dv, tqt), lambda b, h, i: (b, h, 0, i)),
        scratch_shapes=[pltpu.VMEM((n_streams, 1, SUB), F32),
                        pltpu.VMEM((n_streams, dvx, SUB), F32),
                        pltpu.VMEM((2, n_streams, TK, SUB), F32),
                        pltpu.VMEM((2, n_streams, 1, SUB), F32),
                        pltpu.VMEM((2, n_streams, TK, SUB), BF16),
                        pltpu.VMEM((2, n_streams, 1, SUB), F32)],
        compiler_params=_cparams(("parallel", "parallel", "arbitrary")),
        name=("flash_alibi" if alibi else "flash_plain") + ("_meta" if meta_tile else ""),
    )(*args)


def flash_attention(q, kx, vt, *, dv, feat0, sigma=None):
    N = q.shape[2]
    n = N - TAIL
    kw = dict(dv=dv, feat0=feat0)
    o_main = _flash_call(q, kx, vt, sigma, meta_tile=False, n_streams=N_STREAMS, n_q=n, **kw)
    o_meta = _flash_call(q[:, :, n:], kx, vt, sigma, meta_tile=True, n_streams=1, n_q=TAIL, **kw)
    return jnp.concatenate([o_main, o_meta], axis=-1)


def _rms(x, g):
    y = x * lax.rsqrt(jnp.mean(x * x, axis=-1, keepdims=True) + NORM_EPS)
    return y * g


def _rope(x, pos):
    d = x.shape[-1]
    half = d // 2
    inv = ROPE_THETA ** (-2.0 * jnp.arange(half, dtype=F32) / d)
    ang = pos[:, None] * inv[None, :]
    cos = jnp.cos(ang)[:, None, :]
    sin = jnp.sin(ang)[:, None, :]
    x1, x2 = x[..., :half], x[..., half:]
    return jnp.concatenate([x1 * cos - x2 * sin, x2 * cos + x1 * sin], axis=-1)


def _axial_rope(x, row, col):
    h = x.shape[-1] // 2
    return jnp.concatenate([_rope(x[..., :h], row), _rope(x[..., h:], col)], axis=-1)


def _positions(n):
    pad = TAIL - N_META
    rows = n // GRID_W
    row = np.concatenate([np.repeat(np.arange(rows), GRID_W), np.full(N_META, -1.0), np.zeros(pad)])
    col = np.concatenate([np.tile(np.arange(GRID_W), rows), np.arange(N_META), np.zeros(pad)])
    lin = np.concatenate([N_META + np.arange(n), np.arange(N_META), np.zeros(pad)])
    valid = np.concatenate([np.ones(n + N_META), np.zeros(pad)])
    f = lambda a: jnp.asarray(a, F32)
    return f(row), f(col), f(lin), f(valid)


def _heads_first(x):
    return jnp.transpose(x, (0, 2, 1, 3))


def _pad_last(x, width):
    return jnp.pad(x, [(0, 0)] * (x.ndim - 1) + [(0, width - x.shape[-1])])


def _kx(k, valid, width):
    flag = jnp.broadcast_to((1.0 - valid)[None, :, None, None], k.shape[:3] + (1,))
    k = k * valid[None, :, None, None]
    return _heads_first(_pad_last(jnp.concatenate([k, flag], -1), width)).astype(BF16)


def _qx(q, width):
    mask = jnp.full(q.shape[:3] + (1,), MASK_SCORE, F32)
    return _heads_first(_pad_last(jnp.concatenate([q, mask], -1), width)).astype(BF16)


def _vt(v, valid):
    ones = jnp.ones(v.shape[:3] + (1,), F32)
    v = _pad_last(jnp.concatenate([v, ones], -1), v.shape[-1] + ONES_ROWS)
    v = v * valid[None, :, None, None]
    return jnp.transpose(v, (0, 2, 3, 1)).astype(BF16)


def _out_tokens(o):
    return jnp.transpose(o, (0, 3, 1, 2))


def _mixer_gqa(q, k, v, pos, g_q, g_k):
    row, col, _, valid = pos
    B, N, _ = q.shape
    q = _axial_rope(_rms(q.reshape(B, N, A_HEADS, A_HEAD_DIM), g_q), row, col)
    k = _axial_rope(_rms(k.reshape(B, N, A_KV_HEADS, A_HEAD_DIM), g_k), row, col)
    v = v.reshape(B, N, A_KV_HEADS, A_HEAD_DIM)
    q = q * (A_HEAD_DIM ** -0.5 * LOG2E)
    o = flash_attention(_qx(q, 128), _kx(k, valid, 128), _vt(v, valid),
                        dv=A_HEAD_DIM, feat0=A_HEAD_DIM)
    return _out_tokens(o).reshape(B, N, A_HEADS * A_HEAD_DIM)


def _mixer_diff(q, k, v, pos, g_q, g_k, lq1, lk1, lq2, lk2, g_sub, lam_init):
    _, _, lin, valid = pos
    B, N, _ = q.shape
    q = _rms(q.reshape(B, N, 2 * B_HEADS, B_HEAD_DIM), g_q) * (B_HEAD_DIM ** -0.5)
    k = _rms(k.reshape(B, N, 2 * B_HEADS, B_HEAD_DIM), g_k)
    v = v.reshape(B, N, B_HEADS, B_V_DIM)
    lam = (jnp.exp(jnp.sum(lq1 * lk1)) - jnp.exp(jnp.sum(lq2 * lk2)) + lam_init)
    slopes = 2.0 ** (-8.0 * jnp.arange(1, B_HEADS + 1, dtype=F32) / B_HEADS)
    sig = jnp.repeat(slopes, 2)
    idx = np.arange(N)
    ii = jnp.asarray(idx % SUB, F32)
    jj = idx % TK
    one = jnp.ones((N,), F32)
    kfeat = jnp.stack([one, one, jnp.asarray(jj // 256, F32), jnp.asarray(jj % 256, F32)], -1)
    ii_hi = jnp.floor(ii / 256.0) * 256.0
    qfeat = jnp.stack([-(ii_hi)[:, None] * sig[None, :], -(ii - ii_hi)[:, None] * sig[None, :],
                       256.0 * jnp.broadcast_to(sig[None, :], (N, 2 * B_HEADS)),
                       jnp.broadcast_to(sig[None, :], (N, 2 * B_HEADS))], -1)
    mask = jnp.full((B, N, 2 * B_HEADS, 1), MASK_SCORE, F32)
    qx = jnp.concatenate([q, jnp.broadcast_to(qfeat[None], (B,) + qfeat.shape), mask], -1)
    qx = _heads_first(_pad_last(qx, 128)).astype(BF16)
    flag = jnp.broadcast_to((1.0 - valid)[None, :, None, None], (B, N, 2 * B_HEADS, 1))
    kfe = jnp.broadcast_to(kfeat[None, :, None, :], (B, N, 2 * B_HEADS, 4))
    kx = jnp.concatenate([k, kfe, flag], -1) * jnp.concatenate(
        [jnp.broadcast_to(valid[None, :, None, None], (B, N, 2 * B_HEADS, B_HEAD_DIM + 4)),
         jnp.ones((B, N, 2 * B_HEADS, 1), F32)], -1)
    kx = _heads_first(_pad_last(kx, 128)).astype(BF16)
    o = flash_attention(qx, kx, _vt(v, valid), dv=B_V_DIM, feat0=B_HEAD_DIM, sigma=sig)
    o = _out_tokens(o).reshape(B, N, B_HEADS, 2, B_V_DIM)
    o = o[:, :, :, 0] - lam * o[:, :, :, 1]
    o = _rms(o, g_sub) * (1.0 - lam_init)
    return o.reshape(B, N, B_HEADS * B_V_DIM)


def _mixer_mla(cq, ckv, kpe, pos, g_qa, w_qb, g_kva, w_kvb, g_q, g_k):
    _, _, lin, valid = pos
    B, N, _ = cq.shape
    dqk = C_NOPE + C_ROPE
    q = norm_matmul(cq.reshape(B * N, C_Q_LORA), g_qa, w_qb.astype(BF16), tn=C_HEADS * dqk)
    kv = norm_matmul(ckv.reshape(B * N, C_KV_LORA), g_kva, w_kvb.astype(BF16),
                     tn=C_HEADS * (C_NOPE + C_V))
    q = q.reshape(B, N, C_HEADS, dqk)
    kv = kv.reshape(B, N, C_HEADS, C_NOPE + C_V)
    k_nope, v = kv[..., :C_NOPE], kv[..., C_NOPE:]
    kpe = jnp.broadcast_to(kpe[:, :, None, :], (B, N, C_HEADS, C_ROPE))
    k = jnp.concatenate([k_nope, kpe], axis=-1)
    q = _rms(q, g_q)
    k = _rms(k, g_k)
    q = jnp.concatenate([q[..., :C_NOPE], _rope(q[..., C_NOPE:], lin)], axis=-1)
    k = jnp.concatenate([k[..., :C_NOPE], _rope(k[..., C_NOPE:], lin)], axis=-1)
    q = q * (dqk ** -0.5 * LOG2E)
    o = flash_attention(_qx(q, 256), _kx(k, valid, 256), _vt(v, valid), dv=C_V, feat0=dqk)
    return _out_tokens(o).reshape(B, N, C_HEADS * C_V)


def _layer(x, pos, l, p):
    B, N, _ = x.shape
    R = B * N
    x2 = x.reshape(R, D_MODEL)
    w_in = _pad_last(p['w_in'][l], 6144).astype(BF16)
    z = norm_matmul(x2, p['attn_norm_g'][l], w_in, tn=1536).reshape(B, N, -1)
    cuts = [int(c) for c in np.cumsum(IN_SPLITS)]
    parts = [z[..., a:b] for a, b in zip([0] + cuts[:-1], cuts)]
    qa, ka, va, qb, kb, vb, cq, ckv, ckpe, gates = parts
    lam_init = 0.8 - 0.6 * math.exp(-0.3 * l)

    oa = _mixer_gqa(qa, ka, va, pos, p['a_q_norm_g'][l], p['a_k_norm_g'][l])
    ob = _mixer_diff(qb, kb, vb, pos, p['b_q_norm_g'][l], p['b_k_norm_g'][l],
                     p['b_lambda_q1'][l], p['b_lambda_k1'][l], p['b_lambda_q2'][l],
                     p['b_lambda_k2'][l], p['b_subln_g'][l], lam_init)
    oc = _mixer_mla(cq, ckv, ckpe, pos, p['c_q_a_norm_g'][l], p['c_w_q_b'][l],
                    p['c_kv_a_norm_g'][l], p['c_w_kv_b'][l], p['c_q_norm_g'][l],
                    p['c_k_norm_g'][l])

    flat = lambda o: o.reshape(R, -1).astype(BF16)
    x2 = merge_branches(flat(oa), flat(ob), flat(oc), gates.reshape(R, -1), p['b_gate'][l], x2,
                        p['w_branch_a'][l].astype(BF16), p['w_branch_b'][l].astype(BF16),
                        p['w_branch_c'][l].astype(BF16), p['w_out'][l].astype(BF16))
    x2 = mlp_block(x2, p['mlp_norm_g'][l], p['w_up'][l].astype(BF16), p['w_down'][l].astype(BF16))
    return x2.reshape(B, N, D_MODEL)


def _trunk(x, meta_tokens, p):
    B, n, _ = x.shape
    assert n % TK == 0 and n % (N_STREAMS * SUB) == 0 and n % GRID_W == 0
    pos = _positions(n)
    meta = jnp.broadcast_to(meta_tokens[None], (B, N_META, D_MODEL))
    h = jnp.concatenate([x, meta, jnp.zeros((B, TAIL - N_META, D_MODEL), F32)], axis=1)
    for l in range(DEPTH):
        h = _layer(h, pos, l, p)
    return h[:, :n]


def kernel(x_prompt, x_sample, meta_tokens, attn_norm_g, w_in, b_gate, a_q_norm_g, a_k_norm_g, b_q_norm_g, b_k_norm_g, b_lambda_q1, b_lambda_k1, b_lambda_q2, b_lambda_k2, b_subln_g, c_q_a_norm_g, c_w_q_b, c_kv_a_norm_g, c_w_kv_b, c_q_norm_g, c_k_norm_g, w_branch_a, w_branch_b, w_branch_c, w_out, mlp_norm_g, w_up, w_down):
    p = dict(attn_norm_g=attn_norm_g, w_in=w_in, b_gate=b_gate,
             a_q_norm_g=a_q_norm_g, a_k_norm_g=a_k_norm_g,
             b_q_norm_g=b_q_norm_g, b_k_norm_g=b_k_norm_g,
             b_lambda_q1=b_lambda_q1, b_lambda_k1=b_lambda_k1,
             b_lambda_q2=b_lambda_q2, b_lambda_k2=b_lambda_k2, b_subln_g=b_subln_g,
             c_q_a_norm_g=c_q_a_norm_g, c_w_q_b=c_w_q_b,
             c_kv_a_norm_g=c_kv_a_norm_g, c_w_kv_b=c_w_kv_b,
             c_q_norm_g=c_q_norm_g, c_k_norm_g=c_k_norm_g,
             w_branch_a=w_branch_a, w_branch_b=w_branch_b, w_branch_c=w_branch_c,
             w_out=w_out, mlp_norm_g=mlp_norm_g, w_up=w_up, w_down=w_down)
    return (_trunk(x_prompt, meta_tokens, p), _trunk(x_sample, meta_tokens, p))
```

```python
import functools
import math

import numpy as np
import jax
import jax.numpy as jnp
from jax import lax
from jax.experimental import pallas as pl
from jax.experimental.pallas import tpu as pltpu

F32 = jnp.float32
BF16 = jnp.bfloat16

D_MODEL = 1024
DEPTH = 2
N_META = 16
GRID_W = 64
ROPE_THETA = 10000.0
NORM_EPS = 1e-6
A_HEADS, A_KV_HEADS, A_HEAD_DIM = 8, 2, 64
B_HEADS, B_HEAD_DIM = 4, 64
B_V_DIM = 2 * B_HEAD_DIM
C_HEADS, C_NOPE, C_ROPE, C_V = 4, 128, 64, 128
C_Q_LORA, C_KV_LORA = 256, 128
N_BRANCH = 3
D_FF = 4 * D_MODEL
IN_SPLITS = (
    A_HEADS * A_HEAD_DIM, A_KV_HEADS * A_HEAD_DIM, A_KV_HEADS * A_HEAD_DIM,
    2 * B_HEADS * B_HEAD_DIM, 2 * B_HEADS * B_HEAD_DIM, B_HEADS * B_V_DIM,
    C_Q_LORA, C_KV_LORA, C_ROPE,
    N_BRANCH * D_MODEL,
)
IN_COLS = sum(IN_SPLITS)

LOG2E = math.log2(math.e)
SUB = 256
N_STREAMS = 4
N_STREAMS_ALIBI = 2
TK = 512
TAIL = TK
ROW_TILE = 512
ONES_ROWS = 16
MASK_SCORE = -30000.0
VMEM_LIMIT = 52 * 1024 * 1024


def _cparams(sem):
    return pltpu.CompilerParams(dimension_semantics=sem, vmem_limit_bytes=VMEM_LIMIT)


def _norm_matmul_kernel(x_ref, g_ref, w_ref, o_ref):
    x = x_ref[...]
    r = lax.rsqrt(jnp.mean(x * x, axis=-1, keepdims=True) + NORM_EPS)
    h = (x * r * g_ref[...]).astype(BF16)
    o_ref[...] = jnp.dot(h, w_ref[...], preferred_element_type=F32)


def norm_matmul(x, g, w, tn):
    R, K = x.shape
    C = w.shape[1]
    return pl.pallas_call(
        _norm_matmul_kernel,
        out_shape=jax.ShapeDtypeStruct((R, C), F32),
        grid=(R // ROW_TILE, C // tn),
        in_specs=[
            pl.BlockSpec((ROW_TILE, K), lambda i, j: (i, 0)),
            pl.BlockSpec((1, K), lambda i, j: (0, 0)),
            pl.BlockSpec((K, tn), lambda i, j: (0, j)),
        ],
        out_specs=pl.BlockSpec((ROW_TILE, tn), lambda i, j: (i, j)),
        compiler_params=_cparams(("parallel", "arbitrary")),
        name="norm_matmul",
    )(x, g.reshape(1, K), w)


def _merge_kernel(oa_ref, ob_ref, oc_ref, gate_ref, bg_ref, x_ref,
                  wa_ref, wb_ref, wc_ref, wo_ref, out_ref):
    g = jax.nn.sigmoid(gate_ref[...] + bg_ref[...])
    pa = jnp.dot(oa_ref[...], wa_ref[...], preferred_element_type=F32)
    pb = jnp.dot(ob_ref[...], wb_ref[...], preferred_element_type=F32)
    pc = jnp.dot(oc_ref[...], wc_ref[...], preferred_element_type=F32)
    merged = (g[:, :D_MODEL] * pa + g[:, D_MODEL:2 * D_MODEL] * pb
              + g[:, 2 * D_MODEL:] * pc)
    out_ref[...] = x_ref[...] + jnp.dot(merged.astype(BF16), wo_ref[...],
                                        preferred_element_type=F32)


def merge_branches(oa, ob, oc, gates, b_gate, x, wa, wb, wc, wo):
    R = x.shape[0]
    mix = oa.shape[1]
    row = lambda c: pl.BlockSpec((ROW_TILE, c), lambda i: (i, 0))
    full = lambda a: pl.BlockSpec(a.shape, lambda i: (0, 0))
    return pl.pallas_call(
        _merge_kernel,
        out_shape=jax.ShapeDtypeStruct((R, D_MODEL), F32),
        grid=(R // ROW_TILE,),
        in_specs=[row(mix), row(mix), row(mix), row(N_BRANCH * D_MODEL),
                  pl.BlockSpec((1, N_BRANCH * D_MODEL), lambda i: (0, 0)),
                  row(D_MODEL), full(wa), full(wb), full(wc), full(wo)],
        out_specs=row(D_MODEL),
        compiler_params=_cparams(("parallel",)),
        name="merge_branches",
    )(oa, ob, oc, gates, b_gate.reshape(1, -1), x, wa, wb, wc, wo)


def _mlp_kernel(x_ref, g_ref, wu_ref, wd_ref, out_ref, *, ff_chunk):
    x = x_ref[...]
    r = lax.rsqrt(jnp.mean(x * x, axis=-1, keepdims=True) + NORM_EPS)
    h = (x * r * g_ref[...]).astype(BF16)
    acc = x
    for c in range(D_FF // ff_chunk):
        u = jnp.dot(h, wu_ref[:, c * ff_chunk:(c + 1) * ff_chunk],
                    preferred_element_type=F32)
        a = jnp.square(jnp.maximum(u, 0.0)).astype(BF16)
        acc = acc + jnp.dot(a, wd_ref[c * ff_chunk:(c + 1) * ff_chunk, :],
                            preferred_element_type=F32)
    out_ref[...] = acc


def mlp_block(x, g, wu, wd):
    R = x.shape[0]
    return pl.pallas_call(
        functools.partial(_mlp_kernel, ff_chunk=1024),
        out_shape=jax.ShapeDtypeStruct((R, D_MODEL), F32),
        grid=(R // ROW_TILE,),
        in_specs=[
            pl.BlockSpec((ROW_TILE, D_MODEL), lambda i: (i, 0)),
            pl.BlockSpec((1, D_MODEL), lambda i: (0, 0)),
            pl.BlockSpec(wu.shape, lambda i: (0, 0)),
            pl.BlockSpec(wd.shape, lambda i: (0, 0)),
        ],
        out_specs=pl.BlockSpec((ROW_TILE, D_MODEL), lambda i: (i, 0)),
        compiler_params=_cparams(("parallel",)),
        name="mlp_block",
    )(x, g.reshape(1, -1), wu, wd)


def _dot_nt(a, b):
    return lax.dot_general(a, b, (((1,), (1,)), ((), ())), preferred_element_type=F32)


def _attn_kernel(*refs, n_chunks, dv, feat0, alibi, meta_tile, n_streams):
    if alibi:
        sigma_ref, q_ref, kx_ref, vt_ref, o_ref = refs[:5]
    else:
        q_ref, kx_ref, vt_ref, o_ref = refs[:4]
    m_ref, acc_ref, s_ring, bm_ring, p_ring, al_ring, q2_ref = refs[-7:]
    tqt = n_streams * SUB
    qi = pl.program_id(2)
    streams = range(n_streams)

    m_ref[...] = jnp.full(m_ref.shape, -1e30, F32)
    acc_ref[...] = jnp.zeros(acc_ref.shape, F32)

    ex = jnp.exp if alibi else jnp.exp2

    def q_form(st, form):
        if alibi:
            return q2_ref[form, st]
        return q_ref[st * SUB:(st + 1) * SUB, :]

    if alibi:
        for st in streams:
            q = q_ref[st * SUB:(st + 1) * SUB, :]
            lane = lax.broadcasted_iota(jnp.int32, q.shape, 1)
            q2_ref[0, st] = q
            q2_ref[1, st] = jnp.where((lane >= feat0) & (lane < feat0 + 4), -q, q)
        sigma = sigma_ref[pl.program_id(1)]
        if meta_tile:
            n_diag, j_before = 1, jnp.int32(0)
        else:
            n_diag, j_before = max(1, tqt // TK), (qi * tqt) // TK
    else:
        n_diag = 0
    n_blocks = n_chunks + 1
    n_pure = n_blocks - n_diag
    assert n_pure >= 2

    def base_i(st):
        return jnp.int32(0) if meta_tile else N_META + qi * tqt + st * SUB

    def delta(st, b):
        return (base_i(st) - jnp.where(b == n_chunks, 0, N_META + b * TK)).astype(F32)

    def chunk_rows(b):
        if isinstance(b, int):
            return pl.ds(b * TK, TK)
        return pl.ds(pl.multiple_of(b * TK, TK), TK)

    def block(v):
        if not alibi:
            return v, None, False
        if isinstance(v, int) and v >= n_pure:
            return (n_chunks if meta_tile else j_before + (v - n_pure)), None, True
        if meta_tile:
            return v, True, False
        b = v + jnp.where(v >= j_before, n_diag, 0)
        return b, (v >= j_before) & (b < n_chunks), False

    def stage_a(v, slot):
        b, after, diag = block(v)
        kx = kx_ref[chunk_rows(b), :]
        for st in streams:
            if diag:
                c = sigma * delta(st, b)
                s = jnp.minimum(_dot_nt(kx, q_form(st, 0)) - c, _dot_nt(kx, q_form(st, 1)) + c)
            elif alibi:
                s = _dot_nt(kx, q_form(st, 1 if after is True else after.astype(jnp.int32)))
            else:
                s = _dot_nt(kx, q_form(st, 0))
            s_ring[slot, st] = s
            bm_ring[slot, st] = jnp.max(s, axis=0, keepdims=True)

    def stage_b(v, slot):
        b, after, diag = block(v)
        for st in streams:
            shift = 0.0
            if alibi and not diag:
                d = sigma * delta(st, b)
                shift = jnp.where(after, d, -d)
            m_old = m_ref[st]
            m_new = jnp.maximum(m_old, bm_ring[slot, st] + shift)
            al_ring[slot, st] = ex(m_old - m_new)
            p_ring[slot, st] = ex(s_ring[slot, st] - (m_new - shift)).astype(BF16)
            m_ref[st] = m_new

    def stage_c(v, slot):
        b, _, _ = block(v)
        vt = vt_ref[:, chunk_rows(b)]
        for st in streams:
            acc_ref[st] = acc_ref[st] * al_ring[slot, st] + jnp.dot(
                vt, p_ring[slot, st], preferred_element_type=F32)

    stage_a(0, 0)
    stage_a(1, 1)
    stage_b(0, 0)
    n_pairs = (n_pure - 2) // 2

    def pair(tp, carry):
        v = 2 * tp
        stage_a(v + 2, 0)
        stage_b(v + 1, 1)
        stage_c(v, 0)
        stage_a(v + 3, 1)
        stage_b(v + 2, 0)
        stage_c(v + 1, 1)
        return carry

    lax.fori_loop(0, n_pairs, pair, 0)
    for a in range(2 + 2 * n_pairs, n_blocks + 2):
        if a < n_blocks:
            stage_a(a, a % 2)
        if a - 1 < n_blocks:
            stage_b(a - 1, (a - 1) % 2)
        stage_c(a - 2, a % 2)

    for st in streams:
        acc = acc_ref[st]
        o_ref[:, st * SUB:(st + 1) * SUB] = acc[:dv, :] / acc[dv:dv + 1, :]


def _flash_call(q, kx, vt, sigma, *, dv, feat0, meta_tile, n_streams, n_q):
    B, H, _, Kd = q.shape
    G, Gv = kx.shape[1], vt.shape[1]
    N, dvx = kx.shape[2], vt.shape[2]
    n_chunks = (N - TAIL) // TK
    assert n_chunks * TK + TAIL == N
    tqt = n_streams * SUB
    assert n_q % tqt == 0
    alibi = sigma is not None
    in_specs = [
        pl.BlockSpec((None, None, tqt, Kd), lambda b, h, i: (b, h, i, 0)),
        pl.BlockSpec((None, None, N, Kd), lambda b, h, i: (b, h // (H // G), 0, 0)),
        pl.BlockSpec((None, None, dvx, N), lambda b, h, i: (b, h // (H // Gv), 0, 0)),
    ]
    args = [q, kx, vt]
    if alibi:
        in_specs = [pl.BlockSpec(memory_space=pltpu.SMEM)] + in_specs
        args = [sigma] + args
    return pl.pallas_call(
        functools.partial(_attn_kernel, n_chunks=n_chunks, dv=dv, feat0=feat0, alibi=alibi,
                          meta_tile=meta_tile, n_streams=n_streams),
        out_shape=jax.ShapeDtypeStruct((B, H, dv, n_q), F32),
        grid=(B, H, n_q // tqt),
        in_specs=in_specs,
        out_specs=pl.BlockSpec((None, None, dv, tqt), lambda b, h, i: (b, h, 0, i)),
        scratch_shapes=[pltpu.VMEM((n_streams, 1, SUB), F32),
                        pltpu.VMEM((n_streams, dvx, SUB), F32),
                        pltpu.VMEM((2, n_streams, TK, SUB), F32),
                        pltpu.VMEM((2, n_streams, 1, SUB), F32),
                        pltpu.VMEM((2, n_streams, TK, SUB), BF16),
                        pltpu.VMEM((2, n_streams, 1, SUB), F32),
                        pltpu.VMEM((2, n_streams, SUB, Kd), BF16)],
        compiler_params=_cparams(("parallel", "parallel", "arbitrary")),
        name=("flash_alibi" if alibi else "flash_plain") + ("_meta" if meta_tile else ""),
    )(*args)


def flash_attention(q, kx, vt, *, dv, feat0, sigma=None):
    N = q.shape[2]
    n = N - TAIL
    kw = dict(dv=dv, feat0=feat0)
    n_streams = N_STREAMS if sigma is None else N_STREAMS_ALIBI
    o_main = _flash_call(q, kx, vt, sigma, meta_tile=False, n_streams=n_streams, n_q=n, **kw)
    o_meta = _flash_call(q[:, :, n:n + SUB], kx, vt, sigma, meta_tile=True, n_streams=1, n_q=SUB,
                         **kw)
    pad = jnp.zeros(o_main.shape[:3] + (TAIL - SUB,), F32)
    return jnp.concatenate([o_main, o_meta, pad], axis=-1)


def _rms(x, g):
    y = x * lax.rsqrt(jnp.mean(x * x, axis=-1, keepdims=True) + NORM_EPS)
    return y * g


def _rope(x, pos):
    d = x.shape[-1]
    half = d // 2
    inv = ROPE_THETA ** (-2.0 * jnp.arange(half, dtype=F32) / d)
    ang = pos[:, None] * inv[None, :]
    cos = jnp.cos(ang)[:, None, :]
    sin = jnp.sin(ang)[:, None, :]
    x1, x2 = x[..., :half], x[..., half:]
    return jnp.concatenate([x1 * cos - x2 * sin, x2 * cos + x1 * sin], axis=-1)


def _axial_rope(x, row, col):
    h = x.shape[-1] // 2
    return jnp.concatenate([_rope(x[..., :h], row), _rope(x[..., h:], col)], axis=-1)


def _positions(n):
    pad = TAIL - N_META
    rows = n // GRID_W
    row = np.concatenate([np.repeat(np.arange(rows), GRID_W), np.full(N_META, -1.0), np.zeros(pad)])
    col = np.concatenate([np.tile(np.arange(GRID_W), rows), np.arange(N_META), np.zeros(pad)])
    lin = np.concatenate([N_META + np.arange(n), np.arange(N_META), np.zeros(pad)])
    valid = np.concatenate([np.ones(n + N_META), np.zeros(pad)])
    f = lambda a: jnp.asarray(a, F32)
    return f(row), f(col), f(lin), f(valid)


def _heads_first(x):
    return jnp.transpose(x, (0, 2, 1, 3))


def _pad_last(x, width):
    return jnp.pad(x, [(0, 0)] * (x.ndim - 1) + [(0, width - x.shape[-1])])


def _kx(k, valid, width):
    flag = jnp.broadcast_to((1.0 - valid)[None, :, None, None], k.shape[:3] + (1,))
    k = k * valid[None, :, None, None]
    return _heads_first(_pad_last(jnp.concatenate([k, flag], -1), width)).astype(BF16)


def _qx(q, width):
    mask = jnp.full(q.shape[:3] + (1,), MASK_SCORE, F32)
    return _heads_first(_pad_last(jnp.concatenate([q, mask], -1), width)).astype(BF16)


def _vt(v, valid):
    ones = jnp.ones(v.shape[:3] + (1,), F32)
    v = _pad_last(jnp.concatenate([v, ones], -1), v.shape[-1] + ONES_ROWS)
    v = v * valid[None, :, None, None]
    return jnp.transpose(v, (0, 2, 3, 1)).astype(BF16)


def _out_tokens(o):
    return jnp.transpose(o, (0, 3, 1, 2))


def _mixer_gqa(q, k, v, pos, g_q, g_k):
    row, col, _, valid = pos
    B, N, _ = q.shape
    q = _axial_rope(_rms(q.reshape(B, N, A_HEADS, A_HEAD_DIM), g_q), row, col)
    k = _axial_rope(_rms(k.reshape(B, N, A_KV_HEADS, A_HEAD_DIM), g_k), row, col)
    v = v.reshape(B, N, A_KV_HEADS, A_HEAD_DIM)
    q = q * (A_HEAD_DIM ** -0.5 * LOG2E)
    o = flash_attention(_qx(q, 128), _kx(k, valid, 128), _vt(v, valid),
                        dv=A_HEAD_DIM, feat0=A_HEAD_DIM)
    return _out_tokens(o).reshape(B, N, A_HEADS * A_HEAD_DIM)


def _mixer_diff(q, k, v, pos, g_q, g_k, lq1, lk1, lq2, lk2, g_sub, lam_init):
    _, _, lin, valid = pos
    B, N, _ = q.shape
    q = _rms(q.reshape(B, N, 2 * B_HEADS, B_HEAD_DIM), g_q) * (B_HEAD_DIM ** -0.5)
    k = _rms(k.reshape(B, N, 2 * B_HEADS, B_HEAD_DIM), g_k)
    v = v.reshape(B, N, B_HEADS, B_V_DIM)
    lam = (jnp.exp(jnp.sum(lq1 * lk1)) - jnp.exp(jnp.sum(lq2 * lk2)) + lam_init)
    slopes = 2.0 ** (-8.0 * jnp.arange(1, B_HEADS + 1, dtype=F32) / B_HEADS)
    sig = jnp.repeat(slopes, 2)
    idx = np.arange(N)
    ii = jnp.asarray(idx % SUB, F32)
    jj = idx % TK
    one = jnp.ones((N,), F32)
    kfeat = jnp.stack([one, one, jnp.asarray(jj // 256, F32), jnp.asarray(jj % 256, F32)], -1)
    ii_hi = jnp.floor(ii / 256.0) * 256.0
    qfeat = jnp.stack([-(ii_hi)[:, None] * sig[None, :], -(ii - ii_hi)[:, None] * sig[None, :],
                       256.0 * jnp.broadcast_to(sig[None, :], (N, 2 * B_HEADS)),
                       jnp.broadcast_to(sig[None, :], (N, 2 * B_HEADS))], -1)
    mask = jnp.full((B, N, 2 * B_HEADS, 1), MASK_SCORE, F32)
    qx = jnp.concatenate([q, jnp.broadcast_to(qfeat[None], (B,) + qfeat.shape), mask], -1)
    qx = _heads_first(_pad_last(qx, 128)).astype(BF16)
    flag = jnp.broadcast_to((1.0 - valid)[None, :, None, None], (B, N, 2 * B_HEADS, 1))
    kfe = jnp.broadcast_to(kfeat[None, :, None, :], (B, N, 2 * B_HEADS, 4))
    kx = jnp.concatenate([k, kfe, flag], -1) * jnp.concatenate(
        [jnp.broadcast_to(valid[None, :, None, None], (B, N, 2 * B_HEADS, B_HEAD_DIM + 4)),
         jnp.ones((B, N, 2 * B_HEADS, 1), F32)], -1)
    kx = _heads_first(_pad_last(kx, 128)).astype(BF16)
    o = flash_attention(qx, kx, _vt(v, valid), dv=B_V_DIM, feat0=B_HEAD_DIM, sigma=sig)
    o = _out_tokens(o).reshape(B, N, B_HEADS, 2, B_V_DIM)
    o = o[:, :, :, 0] - lam * o[:, :, :, 1]
    o = _rms(o, g_sub) * (1.0 - lam_init)
    return o.reshape(B, N, B_HEADS * B_V_DIM)


def _mixer_mla(cq, ckv, kpe, pos, g_qa, w_qb, g_kva, w_kvb, g_q, g_k):
    _, _, lin, valid = pos
    B, N, _ = cq.shape
    dqk = C_NOPE + C_ROPE
    q = norm_matmul(cq.reshape(B * N, C_Q_LORA), g_qa, w_qb.astype(BF16), tn=C_HEADS * dqk)
    kv = norm_matmul(ckv.reshape(B * N, C_KV_LORA), g_kva, w_kvb.astype(BF16),
                     tn=C_HEADS * (C_NOPE + C_V))
    q = q.reshape(B, N, C_HEADS, dqk)
    kv = kv.reshape(B, N, C_HEADS, C_NOPE + C_V)
    k_nope, v = kv[..., :C_NOPE], kv[..., C_NOPE:]
    kpe = jnp.broadcast_to(kpe[:, :, None, :], (B, N, C_HEADS, C_ROPE))
    k = jnp.concatenate([k_nope, kpe], axis=-1)
    q = _rms(q, g_q)
    k = _rms(k, g_k)
    q = jnp.concatenate([q[..., :C_NOPE], _rope(q[..., C_NOPE:], lin)], axis=-1)
    k = jnp.concatenate([k[..., :C_NOPE], _rope(k[..., C_NOPE:], lin)], axis=-1)
    q = q * (dqk ** -0.5 * LOG2E)
    o = flash_attention(_qx(q, 256), _kx(k, valid, 256), _vt(v, valid), dv=C_V, feat0=dqk)
    return _out_tokens(o).reshape(B, N, C_HEADS * C_V)


def _layer(x, pos, l, p):
    B, N, _ = x.shape
    R = B * N
    x2 = x.reshape(R, D_MODEL)
    w_in = _pad_last(p['w_in'][l], 6144).astype(BF16)
    z = norm_matmul(x2, p['attn_norm_g'][l], w_in, tn=1536).reshape(B, N, -1)
    cuts = [int(c) for c in np.cumsum(IN_SPLITS)]
    parts = [z[..., a:b] for a, b in zip([0] + cuts[:-1], cuts)]
    qa, ka, va, qb, kb, vb, cq, ckv, ckpe, gates = parts
    lam_init = 0.8 - 0.6 * math.exp(-0.3 * l)

    oa = _mixer_gqa(qa, ka, va, pos, p['a_q_norm_g'][l], p['a_k_norm_g'][l])
    ob = _mixer_diff(qb, kb, vb, pos, p['b_q_norm_g'][l], p['b_k_norm_g'][l],
                     p['b_lambda_q1'][l], p['b_lambda_k1'][l], p['b_lambda_q2'][l],
                     p['b_lambda_k2'][l], p['b_subln_g'][l], lam_init)
    oc = _mixer_mla(cq, ckv, ckpe, pos, p['c_q_a_norm_g'][l], p['c_w_q_b'][l],
                    p['c_kv_a_norm_g'][l], p['c_w_kv_b'][l], p['c_q_norm_g'][l],
                    p['c_k_norm_g'][l])

    flat = lambda o: o.reshape(R, -1).astype(BF16)
    x2 = merge_branches(flat(oa), flat(ob), flat(oc), gates.reshape(R, -1), p['b_gate'][l], x2,
                        p['w_branch_a'][l].astype(BF16), p['w_branch_b'][l].astype(BF16),
                        p['w_branch_c'][l].astype(BF16), p['w_out'][l].astype(BF16))
    x2 = mlp_block(x2, p['mlp_norm_g'][l], p['w_up'][l].astype(BF16), p['w_down'][l].astype(BF16))
    return x2.reshape(B, N, D_MODEL)


def _trunk(x, meta_tokens, p):
    B, n, _ = x.shape
    assert n % TK == 0 and n % (N_STREAMS * SUB) == 0 and n % GRID_W == 0
    pos = _positions(n)
    meta = jnp.broadcast_to(meta_tokens[None], (B, N_META, D_MODEL))
    h = jnp.concatenate([x, meta, jnp.zeros((B, TAIL - N_META, D_MODEL), F32)], axis=1)
    for l in range(DEPTH):
        h = _layer(h, pos, l, p)
    return h[:, :n]


def kernel(x_prompt, x_sample, meta_tokens, attn_norm_g, w_in, b_gate, a_q_norm_g, a_k_norm_g, b_q_norm_g, b_k_norm_g, b_lambda_q1, b_lambda_k1, b_lambda_q2, b_lambda_k2, b_subln_g, c_q_a_norm_g, c_w_q_b, c_kv_a_norm_g, c_w_kv_b, c_q_norm_g, c_k_norm_g, w_branch_a, w_branch_b, w_branch_c, w_out, mlp_norm_g, w_up, w_down):
    p = dict(attn_norm_g=attn_norm_g, w_in=w_in, b_gate=b_gate,
             a_q_norm_g=a_q_norm_g, a_k_norm_g=a_k_norm_g,
             b_q_norm_g=b_q_norm_g, b_k_norm_g=b_k_norm_g,
             b_lambda_q1=b_lambda_q1, b_lambda_k1=b_lambda_k1,
             b_lambda_q2=b_lambda_q2, b_lambda_k2=b_lambda_k2, b_subln_g=b_subln_g,
             c_q_a_norm_g=c_q_a_norm_g, c_w_q_b=c_w_q_b,
             c_kv_a_norm_g=c_kv_a_norm_g, c_w_kv_b=c_w_kv_b,
             c_q_norm_g=c_q_norm_g, c_k_norm_g=c_k_norm_g,
             w_branch_a=w_branch_a, w_branch_b=w_branch_b, w_branch_c=w_branch_c,
             w_out=w_out, mlp_norm_g=mlp_norm_g, w_up=w_up, w_down=w_down)
    return (_trunk(x_prompt, meta_tokens, p), _trunk(x_sample, meta_tokens, p))
```

```python
import functools
import math

import numpy as np
import jax
import jax.numpy as jnp
from jax import lax
from jax.experimental import pallas as pl
from jax.experimental.pallas import tpu as pltpu

F32 = jnp.float32
BF16 = jnp.bfloat16

D_MODEL = 1024
DEPTH = 2
N_META = 16
GRID_W = 64
ROPE_THETA = 10000.0
NORM_EPS = 1e-6
A_HEADS, A_KV_HEADS, A_HEAD_DIM = 8, 2, 64
B_HEADS, B_HEAD_DIM = 4, 64
B_V_DIM = 2 * B_HEAD_DIM
C_HEADS, C_NOPE, C_ROPE, C_V = 4, 128, 64, 128
C_Q_LORA, C_KV_LORA = 256, 128
N_BRANCH = 3
D_FF = 4 * D_MODEL
IN_SPLITS = (
    A_HEADS * A_HEAD_DIM, A_KV_HEADS * A_HEAD_DIM, A_KV_HEADS * A_HEAD_DIM,
    2 * B_HEADS * B_HEAD_DIM, 2 * B_HEADS * B_HEAD_DIM, B_HEADS * B_V_DIM,
    C_Q_LORA, C_KV_LORA, C_ROPE,
    N_BRANCH * D_MODEL,
)
IN_COLS = sum(IN_SPLITS)

LOG2E = math.log2(math.e)
SUB = 256
N_STREAMS = 4
N_STREAMS_ALIBI = 2
TK = 512
TAIL = TK
ROW_TILE = 512
ONES_ROWS = 16
MASK_SCORE = -30000.0
EXP_ZERO = -110.0
QK_BOUND_SLACK = 1.02
VMEM_LIMIT = 52 * 1024 * 1024


def _cparams(sem):
    return pltpu.CompilerParams(dimension_semantics=sem, vmem_limit_bytes=VMEM_LIMIT)


def _norm_matmul_kernel(x_ref, g_ref, w_ref, o_ref):
    x = x_ref[...]
    r = lax.rsqrt(jnp.mean(x * x, axis=-1, keepdims=True) + NORM_EPS)
    h = (x * r * g_ref[...]).astype(BF16)
    o_ref[...] = jnp.dot(h, w_ref[...], preferred_element_type=F32)


def norm_matmul(x, g, w, tn):
    R, K = x.shape
    C = w.shape[1]
    return pl.pallas_call(
        _norm_matmul_kernel,
        out_shape=jax.ShapeDtypeStruct((R, C), F32),
        grid=(R // ROW_TILE, C // tn),
        in_specs=[
            pl.BlockSpec((ROW_TILE, K), lambda i, j: (i, 0)),
            pl.BlockSpec((1, K), lambda i, j: (0, 0)),
            pl.BlockSpec((K, tn), lambda i, j: (0, j)),
        ],
        out_specs=pl.BlockSpec((ROW_TILE, tn), lambda i, j: (i, j)),
        compiler_params=_cparams(("parallel", "arbitrary")),
        name="norm_matmul",
    )(x, g.reshape(1, K), w)


def _merge_kernel(oa_ref, ob_ref, oc_ref, gate_ref, bg_ref, x_ref,
                  wa_ref, wb_ref, wc_ref, wo_ref, out_ref):
    g = jax.nn.sigmoid(gate_ref[...] + bg_ref[...])
    pa = jnp.dot(oa_ref[...], wa_ref[...], preferred_element_type=F32)
    pb = jnp.dot(ob_ref[...], wb_ref[...], preferred_element_type=F32)
    pc = jnp.dot(oc_ref[...], wc_ref[...], preferred_element_type=F32)
    merged = (g[:, :D_MODEL] * pa + g[:, D_MODEL:2 * D_MODEL] * pb
              + g[:, 2 * D_MODEL:] * pc)
    out_ref[...] = x_ref[...] + jnp.dot(merged.astype(BF16), wo_ref[...],
                                        preferred_element_type=F32)


def merge_branches(oa, ob, oc, gates, b_gate, x, wa, wb, wc, wo):
    R = x.shape[0]
    mix = oa.shape[1]
    row = lambda c: pl.BlockSpec((ROW_TILE, c), lambda i: (i, 0))
    full = lambda a: pl.BlockSpec(a.shape, lambda i: (0, 0))
    return pl.pallas_call(
        _merge_kernel,
        out_shape=jax.ShapeDtypeStruct((R, D_MODEL), F32),
        grid=(R // ROW_TILE,),
        in_specs=[row(mix), row(mix), row(mix), row(N_BRANCH * D_MODEL),
                  pl.BlockSpec((1, N_BRANCH * D_MODEL), lambda i: (0, 0)),
                  row(D_MODEL), full(wa), full(wb), full(wc), full(wo)],
        out_specs=row(D_MODEL),
        compiler_params=_cparams(("parallel",)),
        name="merge_branches",
    )(oa, ob, oc, gates, b_gate.reshape(1, -1), x, wa, wb, wc, wo)


def _mlp_kernel(x_ref, g_ref, wu_ref, wd_ref, out_ref, *, ff_chunk):
    x = x_ref[...]
    r = lax.rsqrt(jnp.mean(x * x, axis=-1, keepdims=True) + NORM_EPS)
    h = (x * r * g_ref[...]).astype(BF16)
    acc = x
    for c in range(D_FF // ff_chunk):
        u = jnp.dot(h, wu_ref[:, c * ff_chunk:(c + 1) * ff_chunk],
                    preferred_element_type=F32)
        a = jnp.square(jnp.maximum(u, 0.0)).astype(BF16)
        acc = acc + jnp.dot(a, wd_ref[c * ff_chunk:(c + 1) * ff_chunk, :],
                            preferred_element_type=F32)
    out_ref[...] = acc


def mlp_block(x, g, wu, wd):
    R = x.shape[0]
    return pl.pallas_call(
        functools.partial(_mlp_kernel, ff_chunk=1024),
        out_shape=jax.ShapeDtypeStruct((R, D_MODEL), F32),
        grid=(R // ROW_TILE,),
        in_specs=[
            pl.BlockSpec((ROW_TILE, D_MODEL), lambda i: (i, 0)),
            pl.BlockSpec((1, D_MODEL), lambda i: (0, 0)),
            pl.BlockSpec(wu.shape, lambda i: (0, 0)),
            pl.BlockSpec(wd.shape, lambda i: (0, 0)),
        ],
        out_specs=pl.BlockSpec((ROW_TILE, D_MODEL), lambda i: (i, 0)),
        compiler_params=_cparams(("parallel",)),
        name="mlp_block",
    )(x, g.reshape(1, -1), wu, wd)


def _dot_nt(a, b):
    return lax.dot_general(a, b, (((1,), (1,)), ((), ())), preferred_element_type=F32)


def _attn_kernel(*refs, n_chunks, dv, feat0, alibi, meta_tile, n_streams):
    if alibi:
        sigma_ref, reach_ref, q_ref, kx_ref, vt_ref, o_ref = refs[:6]
    else:
        q_ref, kx_ref, vt_ref, o_ref = refs[:4]
    m_ref, acc_ref, s_ring, bm_ring, p_ring, al_ring, q2_ref = refs[-7:]
    tqt = n_streams * SUB
    qi = pl.program_id(2)
    streams = range(n_streams)

    m_ref[...] = jnp.full(m_ref.shape, -1e30, F32)
    acc_ref[...] = jnp.zeros(acc_ref.shape, F32)

    ex = jnp.exp if alibi else jnp.exp2

    def q_form(st, form):
        if alibi:
            return q2_ref[form, st]
        return q_ref[st * SUB:(st + 1) * SUB, :]

    if alibi:
        for st in streams:
            q = q_ref[st * SUB:(st + 1) * SUB, :]
            lane = lax.broadcasted_iota(jnp.int32, q.shape, 1)
            q2_ref[0, st] = q
            q2_ref[1, st] = jnp.where((lane >= feat0) & (lane < feat0 + 4), -q, q)
        sigma = sigma_ref[pl.program_id(1)]
        if meta_tile:
            n_diag, j_before = 1, jnp.int32(0)
        else:
            n_diag, j_before = max(1, tqt // TK), (qi * tqt) // TK
    else:
        n_diag = 0
    n_pure = (n_chunks if meta_tile else n_chunks + 1 - n_diag) if alibi else n_chunks + 1
    if alibi:
        reach = reach_ref[pl.program_id(1)]
        if meta_tile:
            lo, hi = jnp.int32(0), jnp.minimum(n_pure, reach + 1)
        else:
            lo = jnp.maximum(0, j_before - reach)
            hi = jnp.minimum(n_pure, j_before + reach + 2)
        odd = (hi - lo) % 2
        grow_lo = (odd == 1) & (lo > 0)
        lo = lo - jnp.where(grow_lo, 1, 0)
        hi = hi + jnp.where((odd == 1) & jnp.logical_not(grow_lo), 1, 0)
        cnt = hi - lo
        extras = [n_chunks] if meta_tile else [j_before + d for d in range(n_diag)]
    else:
        lo, cnt = 0, n_pure - n_pure % 2
        extras = [n_chunks] if n_pure % 2 else []
    assert n_pure % 2 == 0 or not alibi

    def base_i(st):
        return jnp.int32(0) if meta_tile else N_META + qi * tqt + st * SUB

    def delta(st, b):
        return (base_i(st) - jnp.where(b == n_chunks, 0, N_META + b * TK)).astype(F32)

    def chunk_rows(b):
        if isinstance(b, int):
            return pl.ds(b * TK, TK)
        return pl.ds(pl.multiple_of(b * TK, TK), TK)

    def block(pos, extra=None):
        if extra is not None:
            return extras[extra], None, alibi
        v = lo + pos
        if not alibi:
            return v, None, False
        if meta_tile:
            return v, True, False
        after = v > j_before
        b = jnp.where(v == 0, n_chunks, v - 1 + jnp.where(after, n_diag, 0))
        return b, after, False

    def stage_a(blk, slot):
        b, after, diag = blk
        kx = kx_ref[chunk_rows(b), :]
        for st in streams:
            if diag:
                c = sigma * delta(st, b)
                s = jnp.minimum(_dot_nt(kx, q_form(st, 0)) - c, _dot_nt(kx, q_form(st, 1)) + c)
            elif alibi:
                s = _dot_nt(kx, q_form(st, 1 if after is True else after.astype(jnp.int32)))
            else:
                s = _dot_nt(kx, q_form(st, 0))
            s_ring[slot, st] = s
            bm_ring[slot, st] = jnp.max(s, axis=0, keepdims=True)

    def stage_b(blk, slot):
        b, after, diag = blk
        for st in streams:
            shift = 0.0
            if alibi and not diag:
                d = sigma * delta(st, b)
                shift = jnp.where(after, d, -d)
            m_old = m_ref[st]
            m_new = jnp.maximum(m_old, bm_ring[slot, st] + shift)
            al_ring[slot, st] = ex(m_old - m_new)
            p_ring[slot, st] = ex(s_ring[slot, st] - (m_new - shift)).astype(BF16)
            m_ref[st] = m_new

    def stage_c(blk, slot):
        b, _, _ = blk
        vt = vt_ref[:, chunk_rows(b)]
        for st in streams:
            acc_ref[st] = acc_ref[st] * al_ring[slot, st] + jnp.dot(
                vt, p_ring[slot, st], preferred_element_type=F32)

    stage_a(block(0), 0)
    stage_a(block(1), 1)
    stage_b(block(0), 0)

    def pair(tp, carry):
        w = 2 * tp
        stage_a(block(w + 2), 0)
        stage_b(block(w + 1), 1)
        stage_c(block(w), 0)
        stage_a(block(w + 3), 1)
        stage_b(block(w + 2), 0)
        stage_c(block(w + 1), 1)
        return carry

    lax.fori_loop(0, cnt // 2 - 1, pair, 0)

    def at(k):
        return block(cnt + k) if k < 0 else block(None, extra=k)

    for k in range(len(extras) + 2):
        if k < len(extras):
            stage_a(at(k), k % 2)
        if k - 1 < len(extras):
            stage_b(at(k - 1), (k - 1) % 2)
        stage_c(at(k - 2), k % 2)

    for st in streams:
        acc = acc_ref[st]
        o_ref[:, st * SUB:(st + 1) * SUB] = acc[:dv, :] / acc[dv:dv + 1, :]


def _flash_call(q, kx, vt, sigma, *, dv, feat0, meta_tile, n_streams, n_q):
    B, H, _, Kd = q.shape
    G, Gv = kx.shape[1], vt.shape[1]
    N, dvx = kx.shape[2], vt.shape[2]
    n_chunks = (N - TAIL) // TK
    assert n_chunks * TK + TAIL == N
    tqt = n_streams * SUB
    assert n_q % tqt == 0
    alibi = sigma is not None
    in_specs = [
        pl.BlockSpec((None, None, tqt, Kd), lambda b, h, i: (b, h, i, 0)),
        pl.BlockSpec((None, None, N, Kd), lambda b, h, i: (b, h // (H // G), 0, 0)),
        pl.BlockSpec((None, None, dvx, N), lambda b, h, i: (b, h // (H // Gv), 0, 0)),
    ]
    args = [q, kx, vt]
    if alibi:
        in_specs = [pl.BlockSpec(memory_space=pltpu.SMEM)] * 2 + in_specs
        args = list(sigma) + args
    return pl.pallas_call(
        functools.partial(_attn_kernel, n_chunks=n_chunks, dv=dv, feat0=feat0, alibi=alibi,
                          meta_tile=meta_tile, n_streams=n_streams),
        out_shape=jax.ShapeDtypeStruct((B, H, dv, n_q), F32),
        grid=(B, H, n_q // tqt),
        in_specs=in_specs,
        out_specs=pl.BlockSpec((None, None, dv, tqt), lambda b, h, i: (b, h, 0, i)),
        scratch_shapes=[pltpu.VMEM((n_streams, 1, SUB), F32),
                        pltpu.VMEM((n_streams, dvx, SUB), F32),
                        pltpu.VMEM((2, n_streams, TK, SUB), F32),
                        pltpu.VMEM((2, n_streams, 1, SUB), F32),
                        pltpu.VMEM((2, n_streams, TK, SUB), BF16),
                        pltpu.VMEM((2, n_streams, 1, SUB), F32),
                        pltpu.VMEM((2, n_streams, SUB, Kd), BF16)],
        compiler_params=_cparams(("parallel", "parallel", "arbitrary")),
        name=("flash_alibi" if alibi else "flash_plain") + ("_meta" if meta_tile else ""),
    )(*args)


def flash_attention(q, kx, vt, *, dv, feat0, sigma=None):
    N = q.shape[2]
    n = N - TAIL
    kw = dict(dv=dv, feat0=feat0)
    n_streams = N_STREAMS if sigma is None else N_STREAMS_ALIBI
    o_main = _flash_call(q, kx, vt, sigma, meta_tile=False, n_streams=n_streams, n_q=n, **kw)
    o_meta = _flash_call(q[:, :, n:n + SUB], kx, vt, sigma, meta_tile=True, n_streams=1, n_q=SUB,
                         **kw)
    pad = jnp.zeros(o_main.shape[:3] + (TAIL - SUB,), F32)
    return jnp.concatenate([o_main, o_meta, pad], axis=-1)


def _rms(x, g):
    y = x * lax.rsqrt(jnp.mean(x * x, axis=-1, keepdims=True) + NORM_EPS)
    return y * g


def _rope(x, pos):
    d = x.shape[-1]
    half = d // 2
    inv = ROPE_THETA ** (-2.0 * jnp.arange(half, dtype=F32) / d)
    ang = pos[:, None] * inv[None, :]
    cos = jnp.cos(ang)[:, None, :]
    sin = jnp.sin(ang)[:, None, :]
    x1, x2 = x[..., :half], x[..., half:]
    return jnp.concatenate([x1 * cos - x2 * sin, x2 * cos + x1 * sin], axis=-1)


def _axial_rope(x, row, col):
    h = x.shape[-1] // 2
    return jnp.concatenate([_rope(x[..., :h], row), _rope(x[..., h:], col)], axis=-1)


def _positions(n):
    pad = TAIL - N_META
    rows = n // GRID_W
    row = np.concatenate([np.repeat(np.arange(rows), GRID_W), np.full(N_META, -1.0), np.zeros(pad)])
    col = np.concatenate([np.tile(np.arange(GRID_W), rows), np.arange(N_META), np.zeros(pad)])
    lin = np.concatenate([N_META + np.arange(n), np.arange(N_META), np.zeros(pad)])
    valid = np.concatenate([np.ones(n + N_META), np.zeros(pad)])
    f = lambda a: jnp.asarray(a, F32)
    return f(row), f(col), f(lin), f(valid)


def _heads_first(x):
    return jnp.transpose(x, (0, 2, 1, 3))


def _pad_last(x, width):
    return jnp.pad(x, [(0, 0)] * (x.ndim - 1) + [(0, width - x.shape[-1])])


def _kx(k, valid, width):
    flag = jnp.broadcast_to((1.0 - valid)[None, :, None, None], k.shape[:3] + (1,))
    k = k * valid[None, :, None, None]
    return _heads_first(_pad_last(jnp.concatenate([k, flag], -1), width)).astype(BF16)


def _qx(q, width):
    mask = jnp.full(q.shape[:3] + (1,), MASK_SCORE, F32)
    return _heads_first(_pad_last(jnp.concatenate([q, mask], -1), width)).astype(BF16)


def _vt(v, valid):
    ones = jnp.ones(v.shape[:3] + (1,), F32)
    v = _pad_last(jnp.concatenate([v, ones], -1), v.shape[-1] + ONES_ROWS)
    v = v * valid[None, :, None, None]
    return jnp.transpose(v, (0, 2, 3, 1)).astype(BF16)


def _out_tokens(o):
    return jnp.transpose(o, (0, 3, 1, 2))


def _mixer_gqa(q, k, v, pos, g_q, g_k):
    row, col, _, valid = pos
    B, N, _ = q.shape
    q = _axial_rope(_rms(q.reshape(B, N, A_HEADS, A_HEAD_DIM), g_q), row, col)
    k = _axial_rope(_rms(k.reshape(B, N, A_KV_HEADS, A_HEAD_DIM), g_k), row, col)
    v = v.reshape(B, N, A_KV_HEADS, A_HEAD_DIM)
    q = q * (A_HEAD_DIM ** -0.5 * LOG2E)
    o = flash_attention(_qx(q, 128), _kx(k, valid, 128), _vt(v, valid),
                        dv=A_HEAD_DIM, feat0=A_HEAD_DIM)
    return _out_tokens(o).reshape(B, N, A_HEADS * A_HEAD_DIM)


def _alibi_reach(sigma, qk_bound):
    dist = (2.0 * QK_BOUND_SLACK * qk_bound - EXP_ZERO) / sigma
    return jnp.clip(jnp.floor(dist / TK), 0, 1 << 20).astype(jnp.int32)


def _mixer_diff(q, k, v, pos, g_q, g_k, lq1, lk1, lq2, lk2, g_sub, lam_init):
    _, _, lin, valid = pos
    B, N, _ = q.shape
    q = _rms(q.reshape(B, N, 2 * B_HEADS, B_HEAD_DIM), g_q) * (B_HEAD_DIM ** -0.5)
    k = _rms(k.reshape(B, N, 2 * B_HEADS, B_HEAD_DIM), g_k)
    v = v.reshape(B, N, B_HEADS, B_V_DIM)
    lam = (jnp.exp(jnp.sum(lq1 * lk1)) - jnp.exp(jnp.sum(lq2 * lk2)) + lam_init)
    slopes = 2.0 ** (-8.0 * jnp.arange(1, B_HEADS + 1, dtype=F32) / B_HEADS)
    sig = jnp.repeat(slopes, 2)
    idx = np.arange(N)
    ii = jnp.asarray(idx % SUB, F32)
    jj = idx % TK
    one = jnp.ones((N,), F32)
    kfeat = jnp.stack([one, one, jnp.asarray(jj // 256, F32), jnp.asarray(jj % 256, F32)], -1)
    ii_hi = jnp.floor(ii / 256.0) * 256.0
    qfeat = jnp.stack([-(ii_hi)[:, None] * sig[None, :], -(ii - ii_hi)[:, None] * sig[None, :],
                       256.0 * jnp.broadcast_to(sig[None, :], (N, 2 * B_HEADS)),
                       jnp.broadcast_to(sig[None, :], (N, 2 * B_HEADS))], -1)
    mask = jnp.full((B, N, 2 * B_HEADS, 1), MASK_SCORE, F32)
    qx = jnp.concatenate([q, jnp.broadcast_to(qfeat[None], (B,) + qfeat.shape), mask], -1)
    qx = _heads_first(_pad_last(qx, 128)).astype(BF16)
    flag = jnp.broadcast_to((1.0 - valid)[None, :, None, None], (B, N, 2 * B_HEADS, 1))
    kfe = jnp.broadcast_to(kfeat[None, :, None, :], (B, N, 2 * B_HEADS, 4))
    kx = jnp.concatenate([k, kfe, flag], -1) * jnp.concatenate(
        [jnp.broadcast_to(valid[None, :, None, None], (B, N, 2 * B_HEADS, B_HEAD_DIM + 4)),
         jnp.ones((B, N, 2 * B_HEADS, 1), F32)], -1)
    kx = _heads_first(_pad_last(kx, 128)).astype(BF16)
    reach = _alibi_reach(sig, B_HEAD_DIM ** 0.5 * jnp.max(jnp.abs(g_q)) * jnp.max(jnp.abs(g_k)))
    o = flash_attention(qx, kx, _vt(v, valid), dv=B_V_DIM, feat0=B_HEAD_DIM, sigma=(sig, reach))
    o = _out_tokens(o).reshape(B, N, B_HEADS, 2, B_V_DIM)
    o = o[:, :, :, 0] - lam * o[:, :, :, 1]
    o = _rms(o, g_sub) * (1.0 - lam_init)
    return o.reshape(B, N, B_HEADS * B_V_DIM)


def _mixer_mla(cq, ckv, kpe, pos, g_qa, w_qb, g_kva, w_kvb, g_q, g_k):
    _, _, lin, valid = pos
    B, N, _ = cq.shape
    dqk = C_NOPE + C_ROPE
    q = norm_matmul(cq.reshape(B * N, C_Q_LORA), g_qa, w_qb.astype(BF16), tn=C_HEADS * dqk)
    kv = norm_matmul(ckv.reshape(B * N, C_KV_LORA), g_kva, w_kvb.astype(BF16),
                     tn=C_HEADS * (C_NOPE + C_V))
    q = q.reshape(B, N, C_HEADS, dqk)
    kv = kv.reshape(B, N, C_HEADS, C_NOPE + C_V)
    k_nope, v = kv[..., :C_NOPE], kv[..., C_NOPE:]
    kpe = jnp.broadcast_to(kpe[:, :, None, :], (B, N, C_HEADS, C_ROPE))
    k = jnp.concatenate([k_nope, kpe], axis=-1)
    q = _rms(q, g_q)
    k = _rms(k, g_k)
    q = jnp.concatenate([q[..., :C_NOPE], _rope(q[..., C_NOPE:], lin)], axis=-1)
    k = jnp.concatenate([k[..., :C_NOPE], _rope(k[..., C_NOPE:], lin)], axis=-1)
    q = q * (dqk ** -0.5 * LOG2E)
    o = flash_attention(_qx(q, 256), _kx(k, valid, 256), _vt(v, valid), dv=C_V, feat0=dqk)
    return _out_tokens(o).reshape(B, N, C_HEADS * C_V)


def _layer(x, pos, l, p):
    B, N, _ = x.shape
    R = B * N
    x2 = x.reshape(R, D_MODEL)
    w_in = _pad_last(p['w_in'][l], 6144).astype(BF16)
    z = norm_matmul(x2, p['attn_norm_g'][l], w_in, tn=1536).reshape(B, N, -1)
    cuts = [int(c) for c in np.cumsum(IN_SPLITS)]
    parts = [z[..., a:b] for a, b in zip([0] + cuts[:-1], cuts)]
    qa, ka, va, qb, kb, vb, cq, ckv, ckpe, gates = parts
    lam_init = 0.8 - 0.6 * math.exp(-0.3 * l)

    oa = _mixer_gqa(qa, ka, va, pos, p['a_q_norm_g'][l], p['a_k_norm_g'][l])
    ob = _mixer_diff(qb, kb, vb, pos, p['b_q_norm_g'][l], p['b_k_norm_g'][l],
                     p['b_lambda_q1'][l], p['b_lambda_k1'][l], p['b_lambda_q2'][l],
                     p['b_lambda_k2'][l], p['b_subln_g'][l], lam_init)
    oc = _mixer_mla(cq, ckv, ckpe, pos, p['c_q_a_norm_g'][l], p['c_w_q_b'][l],
                    p['c_kv_a_norm_g'][l], p['c_w_kv_b'][l], p['c_q_norm_g'][l],
                    p['c_k_norm_g'][l])

    flat = lambda o: o.reshape(R, -1).astype(BF16)
    x2 = merge_branches(flat(oa), flat(ob), flat(oc), gates.reshape(R, -1), p['b_gate'][l], x2,
                        p['w_branch_a'][l].astype(BF16), p['w_branch_b'][l].astype(BF16),
                        p['w_branch_c'][l].astype(BF16), p['w_out'][l].astype(BF16))
    x2 = mlp_block(x2, p['mlp_norm_g'][l], p['w_up'][l].astype(BF16), p['w_down'][l].astype(BF16))
    return x2.reshape(B, N, D_MODEL)


def _trunk(x, meta_tokens, p):
    B, n, _ = x.shape
    assert n % TK == 0 and n % (N_STREAMS * SUB) == 0 and n % GRID_W == 0
    pos = _positions(n)
    meta = jnp.broadcast_to(meta_tokens[None], (B, N_META, D_MODEL))
    h = jnp.concatenate([x, meta, jnp.zeros((B, TAIL - N_META, D_MODEL), F32)], axis=1)
    for l in range(DEPTH):
        h = _layer(h, pos, l, p)
    return h[:, :n]


def kernel(x_prompt, x_sample, meta_tokens, attn_norm_g, w_in, b_gate, a_q_norm_g, a_k_norm_g, b_q_norm_g, b_k_norm_g, b_lambda_q1, b_lambda_k1, b_lambda_q2, b_lambda_k2, b_subln_g, c_q_a_norm_g, c_w_q_b, c_kv_a_norm_g, c_w_kv_b, c_q_norm_g, c_k_norm_g, w_branch_a, w_branch_b, w_branch_c, w_out, mlp_norm_g, w_up, w_down):
    p = dict(attn_norm_g=attn_norm_g, w_in=w_in, b_gate=b_gate,
             a_q_norm_g=a_q_norm_g, a_k_norm_g=a_k_norm_g,
             b_q_norm_g=b_q_norm_g, b_k_norm_g=b_k_norm_g,
             b_lambda_q1=b_lambda_q1, b_lambda_k1=b_lambda_k1,
             b_lambda_q2=b_lambda_q2, b_lambda_k2=b_lambda_k2, b_subln_g=b_subln_g,
             c_q_a_norm_g=c_q_a_norm_g, c_w_q_b=c_w_q_b,
             c_kv_a_norm_g=c_kv_a_norm_g, c_w_kv_b=c_w_kv_b,
             c_q_norm_g=c_q_norm_g, c_k_norm_g=c_k_norm_g,
             w_branch_a=w_branch_a, w_branch_b=w_branch_b, w_branch_c=w_branch_c,
             w_out=w_out, mlp_norm_g=mlp_norm_g, w_up=w_up, w_down=w_down)
    return (_trunk(x_prompt, meta_tokens, p), _trunk(x_sample, meta_tokens, p))
```

```python
import functools
import math

import numpy as np
import jax
import jax.numpy as jnp
from jax import lax
from jax.experimental import pallas as pl
from jax.experimental.pallas import tpu as pltpu

F32 = jnp.float32
BF16 = jnp.bfloat16

D_MODEL = 1024
DEPTH = 2
N_META = 16
GRID_W = 64
ROPE_THETA = 10000.0
NORM_EPS = 1e-6
A_HEADS, A_KV_HEADS, A_HEAD_DIM = 8, 2, 64
B_HEADS, B_HEAD_DIM = 4, 64
B_V_DIM = 2 * B_HEAD_DIM
C_HEADS, C_NOPE, C_ROPE, C_V = 4, 128, 64, 128
C_Q_LORA, C_KV_LORA = 256, 128
N_BRANCH = 3
D_FF = 4 * D_MODEL
IN_SPLITS = (
    A_HEADS * A_HEAD_DIM, A_KV_HEADS * A_HEAD_DIM, A_KV_HEADS * A_HEAD_DIM,
    2 * B_HEADS * B_HEAD_DIM, 2 * B_HEADS * B_HEAD_DIM, B_HEADS * B_V_DIM,
    C_Q_LORA, C_KV_LORA, C_ROPE,
    N_BRANCH * D_MODEL,
)
IN_COLS = sum(IN_SPLITS)

LOG2E = math.log2(math.e)
SUB = 256
N_STREAMS = 4
N_STREAMS_ALIBI = 2
TK = 512
TAIL = TK
ROW_TILE = 512
ONES_ROWS = 16
MASK_SCORE = -30000.0
EXP_ZERO = -110.0
QK_BOUND_SLACK = 1.02
VMEM_LIMIT = 52 * 1024 * 1024


def _cparams(sem):
    return pltpu.CompilerParams(dimension_semantics=sem, vmem_limit_bytes=VMEM_LIMIT)


def _norm_matmul_kernel(x_ref, g_ref, w_ref, o_ref):
    x = x_ref[...]
    r = lax.rsqrt(jnp.mean(x * x, axis=-1, keepdims=True) + NORM_EPS)
    h = (x * r * g_ref[...]).astype(BF16)
    o_ref[...] = jnp.dot(h, w_ref[...], preferred_element_type=F32)


def norm_matmul(x, g, w, tn):
    R, K = x.shape
    C = w.shape[1]
    return pl.pallas_call(
        _norm_matmul_kernel,
        out_shape=jax.ShapeDtypeStruct((R, C), F32),
        grid=(R // ROW_TILE, C // tn),
        in_specs=[
            pl.BlockSpec((ROW_TILE, K), lambda i, j: (i, 0)),
            pl.BlockSpec((1, K), lambda i, j: (0, 0)),
            pl.BlockSpec((K, tn), lambda i, j: (0, j)),
        ],
        out_specs=pl.BlockSpec((ROW_TILE, tn), lambda i, j: (i, j)),
        compiler_params=_cparams(("parallel", "arbitrary")),
        name="norm_matmul",
    )(x, g.reshape(1, K), w)


def _dot_tn(a, w):
    return lax.dot_general(a.astype(BF16), w, (((0,), (0,)), ((), ())),
                           preferred_element_type=F32)


def _merge_kernel(lam_ref, oa_ref, ob_ref, oc_ref, gate_ref, bg_ref, gsub_ref, x_ref,
                  wa_ref, wb_ref, wc_ref, wo_ref, out_ref, *, n_valid, lam_init):
    pa = _dot_tn(oa_ref[...], wa_ref[...])
    heads = []
    for h in range(B_HEADS):
        o1 = ob_ref[(2 * h) * B_V_DIM:(2 * h + 1) * B_V_DIM, :]
        o2 = ob_ref[(2 * h + 1) * B_V_DIM:(2 * h + 2) * B_V_DIM, :]
        d = o1 - lam_ref[0] * o2
        r = lax.rsqrt(jnp.mean(d * d, axis=0, keepdims=True) + NORM_EPS)
        heads.append(d * r * gsub_ref[...] * (1.0 - lam_init))
    pb = _dot_tn(jnp.concatenate(heads, axis=0), wb_ref[...])
    pc = _dot_tn(oc_ref[...], wc_ref[...])
    g = jax.nn.sigmoid(gate_ref[...] + bg_ref[...])
    merged = (g[:, :D_MODEL] * pa + g[:, D_MODEL:2 * D_MODEL] * pb
              + g[:, 2 * D_MODEL:] * pc)
    y = x_ref[...] + jnp.dot(merged.astype(BF16), wo_ref[...], preferred_element_type=F32)
    rows = pl.program_id(1) * ROW_TILE + lax.broadcasted_iota(jnp.int32, (ROW_TILE, 1), 0)
    out_ref[...] = jnp.where(rows < n_valid, y, 0.0)


def merge_branches(lam, oa, ob, oc, z, b_gate, g_sub, x, wa, wb, wc, wo, *, n_valid, lam_init):
    B, N, _ = x.shape
    col = lambda a: pl.BlockSpec((None, a.shape[1], ROW_TILE), lambda b, i: (b, 0, i))
    row = lambda c: pl.BlockSpec((None, ROW_TILE, c), lambda b, i: (b, i, 0))
    full = lambda a: pl.BlockSpec(a.shape, lambda b, i: (0, 0))
    return pl.pallas_call(
        functools.partial(_merge_kernel, n_valid=n_valid, lam_init=lam_init),
        out_shape=jax.ShapeDtypeStruct((B, N, D_MODEL), F32),
        grid=(B, N // ROW_TILE),
        in_specs=[pl.BlockSpec(memory_space=pltpu.SMEM), col(oa), col(ob), col(oc),
                  row(N_BRANCH * D_MODEL),
                  pl.BlockSpec((1, N_BRANCH * D_MODEL), lambda b, i: (0, 0)),
                  pl.BlockSpec((B_V_DIM, 1), lambda b, i: (0, 0)),
                  row(D_MODEL), full(wa), full(wb), full(wc), full(wo)],
        out_specs=row(D_MODEL),
        compiler_params=_cparams(("parallel", "parallel")),
        name="merge_branches",
    )(lam.reshape(1), oa, ob, oc, z, b_gate.reshape(1, -1), g_sub.reshape(-1, 1), x,
      wa, wb, wc, wo)


def _mlp_kernel(x_ref, g_ref, wu_ref, wd_ref, out_ref, *, ff_chunk):
    x = x_ref[...]
    r = lax.rsqrt(jnp.mean(x * x, axis=-1, keepdims=True) + NORM_EPS)
    h = (x * r * g_ref[...]).astype(BF16)
    acc = x
    for c in range(D_FF // ff_chunk):
        u = jnp.dot(h, wu_ref[:, c * ff_chunk:(c + 1) * ff_chunk],
                    preferred_element_type=F32)
        a = jnp.square(jnp.maximum(u, 0.0)).astype(BF16)
        acc = acc + jnp.dot(a, wd_ref[c * ff_chunk:(c + 1) * ff_chunk, :],
                            preferred_element_type=F32)
    out_ref[...] = acc


def mlp_block(x, g, wu, wd):
    R = x.shape[0]
    return pl.pallas_call(
        functools.partial(_mlp_kernel, ff_chunk=1024),
        out_shape=jax.ShapeDtypeStruct((R, D_MODEL), F32),
        grid=(R // ROW_TILE,),
        in_specs=[
            pl.BlockSpec((ROW_TILE, D_MODEL), lambda i: (i, 0)),
            pl.BlockSpec((1, D_MODEL), lambda i: (0, 0)),
            pl.BlockSpec(wu.shape, lambda i: (0, 0)),
            pl.BlockSpec(wd.shape, lambda i: (0, 0)),
        ],
        out_specs=pl.BlockSpec((ROW_TILE, D_MODEL), lambda i: (i, 0)),
        compiler_params=_cparams(("parallel",)),
        name="mlp_block",
    )(x, g.reshape(1, -1), wu, wd)


def _dot_nt(a, b):
    return lax.dot_general(a, b, (((1,), (1,)), ((), ())), preferred_element_type=F32)


def _attn_kernel(*refs, n_chunks, dv, feat0, alibi, meta_tile, n_streams):
    if alibi:
        sigma_ref, reach_ref, q_ref, kx_ref, vt_ref = refs[:5]
    else:
        q_ref, kx_ref, vt_ref = refs[:3]
    o_ref = refs[-8]
    m_ref, acc_ref, s_ring, bm_ring, p_ring, al_ring, q2_ref = refs[-7:]
    tqt = n_streams * SUB
    qi = pl.program_id(2)
    streams = range(n_streams)

    m_ref[...] = jnp.full(m_ref.shape, -1e30, F32)
    acc_ref[...] = jnp.zeros(acc_ref.shape, F32)

    ex = jnp.exp if alibi else jnp.exp2

    def q_form(st, form):
        if alibi:
            return q2_ref[form, st]
        return q_ref[st * SUB:(st + 1) * SUB, :]

    if alibi:
        for st in streams:
            q = q_ref[st * SUB:(st + 1) * SUB, :]
            lane = lax.broadcasted_iota(jnp.int32, q.shape, 1)
            q2_ref[0, st] = q
            q2_ref[1, st] = jnp.where((lane >= feat0) & (lane < feat0 + 4), -q, q)
        sigma = sigma_ref[pl.program_id(1)]
        if meta_tile:
            n_diag, j_before = 1, jnp.int32(0)
        else:
            n_diag, j_before = max(1, tqt // TK), (qi * tqt) // TK
    else:
        n_diag = 0
    n_pure = (n_chunks if meta_tile else n_chunks + 1 - n_diag) if alibi else n_chunks + 1
    if alibi:
        reach = reach_ref[pl.program_id(1)]
        if meta_tile:
            lo, hi = jnp.int32(0), jnp.minimum(n_pure, reach + 1)
        else:
            lo = jnp.maximum(0, j_before - reach)
            hi = jnp.minimum(n_pure, j_before + reach + 2)
        odd = (hi - lo) % 2
        grow_lo = (odd == 1) & (lo > 0)
        lo = lo - jnp.where(grow_lo, 1, 0)
        hi = hi + jnp.where((odd == 1) & jnp.logical_not(grow_lo), 1, 0)
        cnt = hi - lo
        extras = [n_chunks] if meta_tile else [j_before + d for d in range(n_diag)]
    else:
        lo, cnt = 0, n_pure - n_pure % 2
        extras = [n_chunks] if n_pure % 2 else []
    assert n_pure % 2 == 0 or not alibi

    def base_i(st):
        return jnp.int32(0) if meta_tile else N_META + qi * tqt + st * SUB

    def delta(st, b):
        return (base_i(st) - jnp.where(b == n_chunks, 0, N_META + b * TK)).astype(F32)

    def chunk_rows(b):
        if isinstance(b, int):
            return pl.ds(b * TK, TK)
        return pl.ds(pl.multiple_of(b * TK, TK), TK)

    def block(pos, extra=None):
        if extra is not None:
            return extras[extra], None, alibi
        v = lo + pos
        if not alibi:
            return v, None, False
        if meta_tile:
            return v, True, False
        after = v > j_before
        b = jnp.where(v == 0, n_chunks, v - 1 + jnp.where(after, n_diag, 0))
        return b, after, False

    def stage_a(blk, slot):
        b, after, diag = blk
        kx = kx_ref[chunk_rows(b), :]
        for st in streams:
            if diag:
                c = sigma * delta(st, b)
                s = jnp.minimum(_dot_nt(kx, q_form(st, 0)) - c, _dot_nt(kx, q_form(st, 1)) + c)
            elif alibi:
                s = _dot_nt(kx, q_form(st, 1 if after is True else after.astype(jnp.int32)))
            else:
                s = _dot_nt(kx, q_form(st, 0))
            s_ring[slot, st] = s
            bm_ring[slot, st] = jnp.max(s, axis=0, keepdims=True)

    def stage_b(blk, slot):
        b, after, diag = blk
        for st in streams:
            shift = 0.0
            if alibi and not diag:
                d = sigma * delta(st, b)
                shift = jnp.where(after, d, -d)
            m_old = m_ref[st]
            m_new = jnp.maximum(m_old, bm_ring[slot, st] + shift)
            al_ring[slot, st] = ex(m_old - m_new)
            p_ring[slot, st] = ex(s_ring[slot, st] - (m_new - shift)).astype(BF16)
            m_ref[st] = m_new

    def stage_c(blk, slot):
        b, _, _ = blk
        vt = vt_ref[:, chunk_rows(b)]
        for st in streams:
            acc_ref[st] = acc_ref[st] * al_ring[slot, st] + jnp.dot(
                vt, p_ring[slot, st], preferred_element_type=F32)

    stage_a(block(0), 0)
    stage_a(block(1), 1)
    stage_b(block(0), 0)

    def pair(tp, carry):
        w = 2 * tp
        stage_a(block(w + 2), 0)
        stage_b(block(w + 1), 1)
        stage_c(block(w), 0)
        stage_a(block(w + 3), 1)
        stage_b(block(w + 2), 0)
        stage_c(block(w + 1), 1)
        return carry

    lax.fori_loop(0, cnt // 2 - 1, pair, 0)

    def at(k):
        return block(cnt + k) if k < 0 else block(None, extra=k)

    for k in range(len(extras) + 2):
        if k < len(extras):
            stage_a(at(k), k % 2)
        if k - 1 < len(extras):
            stage_b(at(k - 1), (k - 1) % 2)
        stage_c(at(k - 2), k % 2)

    for st in streams:
        acc = acc_ref[st]
        o_ref[:, st * SUB:(st + 1) * SUB] = acc[:dv, :] / acc[dv:dv + 1, :]


def _flash_call(q, kx, vt, sigma, *, dv, feat0, meta_tile, n_streams, q0, n_q, prev=None):
    B, H, _, Kd = q.shape
    G, Gv = kx.shape[1], vt.shape[1]
    N, dvx = kx.shape[2], vt.shape[2]
    n_chunks = (N - TAIL) // TK
    assert n_chunks * TK + TAIL == N
    tqt = n_streams * SUB
    assert n_q % tqt == 0 and q0 % tqt == 0
    t0 = q0 // tqt
    alibi = sigma is not None
    in_specs = [
        pl.BlockSpec((None, None, tqt, Kd), lambda b, h, i: (b, h, t0 + i, 0)),
        pl.BlockSpec((None, None, N, Kd), lambda b, h, i: (b, h // (H // G), 0, 0)),
        pl.BlockSpec((None, None, dvx, N), lambda b, h, i: (b, h // (H // Gv), 0, 0)),
    ]
    args = [q, kx, vt]
    if alibi:
        in_specs = [pl.BlockSpec(memory_space=pltpu.SMEM)] * 2 + in_specs
        args = list(sigma) + args
    aliases = {}
    if prev is not None:
        aliases = {len(args): 0}
        in_specs = in_specs + [pl.BlockSpec(memory_space=pl.ANY)]
        args = args + [prev]
    return pl.pallas_call(
        functools.partial(_attn_kernel, n_chunks=n_chunks, dv=dv, feat0=feat0, alibi=alibi,
                          meta_tile=meta_tile, n_streams=n_streams),
        out_shape=jax.ShapeDtypeStruct((B, H, dv, N), F32),
        grid=(B, H, n_q // tqt),
        in_specs=in_specs,
        out_specs=pl.BlockSpec((None, None, dv, tqt), lambda b, h, i: (b, h, 0, t0 + i)),
        input_output_aliases=aliases,
        scratch_shapes=[pltpu.VMEM((n_streams, 1, SUB), F32),
                        pltpu.VMEM((n_streams, dvx, SUB), F32),
                        pltpu.VMEM((2, n_streams, TK, SUB), F32),
                        pltpu.VMEM((2, n_streams, 1, SUB), F32),
                        pltpu.VMEM((2, n_streams, TK, SUB), BF16),
                        pltpu.VMEM((2, n_streams, 1, SUB), F32),
                        pltpu.VMEM((2, n_streams, SUB, Kd), BF16)],
        compiler_params=_cparams(("parallel", "parallel", "arbitrary")),
        name=("flash_alibi" if alibi else "flash_plain") + ("_meta" if meta_tile else ""),
    )(*args)


def flash_attention(q, kx, vt, *, dv, feat0, sigma=None):
    N = q.shape[2]
    n = N - TAIL
    kw = dict(dv=dv, feat0=feat0)
    n_streams = N_STREAMS if sigma is None else N_STREAMS_ALIBI
    o = _flash_call(q, kx, vt, sigma, meta_tile=False, n_streams=n_streams, q0=0, n_q=n, **kw)
    o = _flash_call(q, kx, vt, sigma, meta_tile=True, n_streams=1, q0=n, n_q=SUB, prev=o, **kw)
    return o.reshape(o.shape[0], -1, N)


def _rms(x, g):
    y = x * lax.rsqrt(jnp.mean(x * x, axis=-1, keepdims=True) + NORM_EPS)
    return y * g


def _rope(x, pos):
    d = x.shape[-1]
    half = d // 2
    inv = ROPE_THETA ** (-2.0 * jnp.arange(half, dtype=F32) / d)
    ang = pos[:, None] * inv[None, :]
    cos = jnp.cos(ang)[:, None, :]
    sin = jnp.sin(ang)[:, None, :]
    x1, x2 = x[..., :half], x[..., half:]
    return jnp.concatenate([x1 * cos - x2 * sin, x2 * cos + x1 * sin], axis=-1)


def _axial_rope(x, row, col):
    h = x.shape[-1] // 2
    return jnp.concatenate([_rope(x[..., :h], row), _rope(x[..., h:], col)], axis=-1)


def _positions(n):
    pad = TAIL - N_META
    rows = n // GRID_W
    row = np.concatenate([np.repeat(np.arange(rows), GRID_W), np.full(N_META, -1.0), np.zeros(pad)])
    col = np.concatenate([np.tile(np.arange(GRID_W), rows), np.arange(N_META), np.zeros(pad)])
    lin = np.concatenate([N_META + np.arange(n), np.arange(N_META), np.zeros(pad)])
    valid = np.concatenate([np.ones(n + N_META), np.zeros(pad)])
    f = lambda a: jnp.asarray(a, F32)
    return f(row), f(col), f(lin), f(valid)


def _heads_first(x):
    return jnp.transpose(x, (0, 2, 1, 3))


def _pad_last(x, width):
    return jnp.pad(x, [(0, 0)] * (x.ndim - 1) + [(0, width - x.shape[-1])])


def _kx(k, valid, width):
    flag = jnp.broadcast_to((1.0 - valid)[None, :, None, None], k.shape[:3] + (1,))
    k = k * valid[None, :, None, None]
    return _heads_first(_pad_last(jnp.concatenate([k, flag], -1), width)).astype(BF16)


def _qx(q, width):
    mask = jnp.full(q.shape[:3] + (1,), MASK_SCORE, F32)
    return _heads_first(_pad_last(jnp.concatenate([q, mask], -1), width)).astype(BF16)


def _vt(v, valid):
    ones = jnp.ones(v.shape[:3] + (1,), F32)
    v = _pad_last(jnp.concatenate([v, ones], -1), v.shape[-1] + ONES_ROWS)
    v = v * valid[None, :, None, None]
    return jnp.transpose(v, (0, 2, 3, 1)).astype(BF16)


def _mixer_gqa(q, k, v, pos, g_q, g_k):
    row, col, _, valid = pos
    B, N, _ = q.shape
    q = _axial_rope(_rms(q.reshape(B, N, A_HEADS, A_HEAD_DIM), g_q), row, col)
    k = _axial_rope(_rms(k.reshape(B, N, A_KV_HEADS, A_HEAD_DIM), g_k), row, col)
    v = v.reshape(B, N, A_KV_HEADS, A_HEAD_DIM)
    q = q * (A_HEAD_DIM ** -0.5 * LOG2E)
    return flash_attention(_qx(q, 128), _kx(k, valid, 128), _vt(v, valid),
                           dv=A_HEAD_DIM, feat0=A_HEAD_DIM)


def _alibi_reach(sigma, qk_bound):
    dist = (2.0 * QK_BOUND_SLACK * qk_bound - EXP_ZERO) / sigma
    return jnp.clip(jnp.floor(dist / TK), 0, 1 << 20).astype(jnp.int32)


def _mixer_diff(q, k, v, pos, g_q, g_k):
    _, _, lin, valid = pos
    B, N, _ = q.shape
    q = _rms(q.reshape(B, N, 2 * B_HEADS, B_HEAD_DIM), g_q) * (B_HEAD_DIM ** -0.5)
    k = _rms(k.reshape(B, N, 2 * B_HEADS, B_HEAD_DIM), g_k)
    v = v.reshape(B, N, B_HEADS, B_V_DIM)
    slopes = 2.0 ** (-8.0 * jnp.arange(1, B_HEADS + 1, dtype=F32) / B_HEADS)
    sig = jnp.repeat(slopes, 2)
    idx = np.arange(N)
    ii = jnp.asarray(idx % SUB, F32)
    jj = idx % TK
    one = jnp.ones((N,), F32)
    kfeat = jnp.stack([one, one, jnp.asarray(jj // 256, F32), jnp.asarray(jj % 256, F32)], -1)
    ii_hi = jnp.floor(ii / 256.0) * 256.0
    qfeat = jnp.stack([-(ii_hi)[:, None] * sig[None, :], -(ii - ii_hi)[:, None] * sig[None, :],
                       256.0 * jnp.broadcast_to(sig[None, :], (N, 2 * B_HEADS)),
                       jnp.broadcast_to(sig[None, :], (N, 2 * B_HEADS))], -1)
    mask = jnp.full((B, N, 2 * B_HEADS, 1), MASK_SCORE, F32)
    qx = jnp.concatenate([q, jnp.broadcast_to(qfeat[None], (B,) + qfeat.shape), mask], -1)
    qx = _heads_first(_pad_last(qx, 128)).astype(BF16)
    flag = jnp.broadcast_to((1.0 - valid)[None, :, None, None], (B, N, 2 * B_HEADS, 1))
    kfe = jnp.broadcast_to(kfeat[None, :, None, :], (B, N, 2 * B_HEADS, 4))
    kx = jnp.concatenate([k, kfe, flag], -1) * jnp.concatenate(
        [jnp.broadcast_to(valid[None, :, None, None], (B, N, 2 * B_HEADS, B_HEAD_DIM + 4)),
         jnp.ones((B, N, 2 * B_HEADS, 1), F32)], -1)
    kx = _heads_first(_pad_last(kx, 128)).astype(BF16)
    reach = _alibi_reach(sig, B_HEAD_DIM ** 0.5 * jnp.max(jnp.abs(g_q)) * jnp.max(jnp.abs(g_k)))
    return flash_attention(qx, kx, _vt(v, valid), dv=B_V_DIM, feat0=B_HEAD_DIM, sigma=(sig, reach))


def _mixer_mla(cq, ckv, kpe, pos, g_qa, w_qb, g_kva, w_kvb, g_q, g_k):
    _, _, lin, valid = pos
    B, N, _ = cq.shape
    dqk = C_NOPE + C_ROPE
    q = norm_matmul(cq.reshape(B * N, C_Q_LORA), g_qa, w_qb.astype(BF16), tn=C_HEADS * dqk)
    kv = norm_matmul(ckv.reshape(B * N, C_KV_LORA), g_kva, w_kvb.astype(BF16),
                     tn=C_HEADS * (C_NOPE + C_V))
    q = q.reshape(B, N, C_HEADS, dqk)
    kv = kv.reshape(B, N, C_HEADS, C_NOPE + C_V)
    k_nope, v = kv[..., :C_NOPE], kv[..., C_NOPE:]
    kpe = jnp.broadcast_to(kpe[:, :, None, :], (B, N, C_HEADS, C_ROPE))
    k = jnp.concatenate([k_nope, kpe], axis=-1)
    q = _rms(q, g_q)
    k = _rms(k, g_k)
    q = jnp.concatenate([q[..., :C_NOPE], _rope(q[..., C_NOPE:], lin)], axis=-1)
    k = jnp.concatenate([k[..., :C_NOPE], _rope(k[..., C_NOPE:], lin)], axis=-1)
    q = q * (dqk ** -0.5 * LOG2E)
    return flash_attention(_qx(q, 256), _kx(k, valid, 256), _vt(v, valid), dv=C_V, feat0=dqk)


def _layer(x, pos, l, p):
    B, N, _ = x.shape
    R = B * N
    x2 = x.reshape(R, D_MODEL)
    n_gate = N_BRANCH * D_MODEL
    w_in = p['w_in'][l]
    w_in = jnp.concatenate([w_in[:, IN_COLS - n_gate:], w_in[:, :IN_COLS - n_gate]], axis=1)
    w_in = _pad_last(w_in, 6144).astype(BF16)
    z = norm_matmul(x2, p['attn_norm_g'][l], w_in, tn=1536).reshape(B, N, -1)
    cuts = [n_gate + int(c) for c in np.cumsum(IN_SPLITS[:-1])]
    parts = [z[..., a:b] for a, b in zip([n_gate] + cuts[:-1], cuts)]
    qa, ka, va, qb, kb, vb, cq, ckv, ckpe = parts
    lam_init = 0.8 - 0.6 * math.exp(-0.3 * l)
    lam = (jnp.exp(jnp.sum(p['b_lambda_q1'][l] * p['b_lambda_k1'][l]))
           - jnp.exp(jnp.sum(p['b_lambda_q2'][l] * p['b_lambda_k2'][l])) + lam_init)

    oa = _mixer_gqa(qa, ka, va, pos, p['a_q_norm_g'][l], p['a_k_norm_g'][l])
    ob = _mixer_diff(qb, kb, vb, pos, p['b_q_norm_g'][l], p['b_k_norm_g'][l])
    oc = _mixer_mla(cq, ckv, ckpe, pos, p['c_q_a_norm_g'][l], p['c_w_q_b'][l],
                    p['c_kv_a_norm_g'][l], p['c_w_kv_b'][l], p['c_q_norm_g'][l],
                    p['c_k_norm_g'][l])

    x = merge_branches(lam, oa, ob, oc, z, p['b_gate'][l], p['b_subln_g'][l], x,
                       p['w_branch_a'][l].astype(BF16), p['w_branch_b'][l].astype(BF16),
                       p['w_branch_c'][l].astype(BF16), p['w_out'][l].astype(BF16),
                       n_valid=N - TAIL + N_META, lam_init=lam_init)
    x2 = mlp_block(x.reshape(R, D_MODEL), p['mlp_norm_g'][l], p['w_up'][l].astype(BF16),
                   p['w_down'][l].astype(BF16))
    return x2.reshape(B, N, D_MODEL)


def _trunk(x, meta_tokens, p):
    B, n, _ = x.shape
    assert n % TK == 0 and n % (N_STREAMS * SUB) == 0 and n % GRID_W == 0
    pos = _positions(n)
    meta = jnp.broadcast_to(meta_tokens[None], (B, N_META, D_MODEL))
    h = jnp.concatenate([x, meta, jnp.zeros((B, TAIL - N_META, D_MODEL), F32)], axis=1)
    for l in range(DEPTH):
        h = _layer(h, pos, l, p)
    return h[:, :n]


def kernel(x_prompt, x_sample, meta_tokens, attn_norm_g, w_in, b_gate, a_q_norm_g, a_k_norm_g, b_q_norm_g, b_k_norm_g, b_lambda_q1, b_lambda_k1, b_lambda_q2, b_lambda_k2, b_subln_g, c_q_a_norm_g, c_w_q_b, c_kv_a_norm_g, c_w_kv_b, c_q_norm_g, c_k_norm_g, w_branch_a, w_branch_b, w_branch_c, w_out, mlp_norm_g, w_up, w_down):
    p = dict(attn_norm_g=attn_norm_g, w_in=w_in, b_gate=b_gate,
             a_q_norm_g=a_q_norm_g, a_k_norm_g=a_k_norm_g,
             b_q_norm_g=b_q_norm_g, b_k_norm_g=b_k_norm_g,
             b_lambda_q1=b_lambda_q1, b_lambda_k1=b_lambda_k1,
             b_lambda_q2=b_lambda_q2, b_lambda_k2=b_lambda_k2, b_subln_g=b_subln_g,
             c_q_a_norm_g=c_q_a_norm_g, c_w_q_b=c_w_q_b,
             c_kv_a_norm_g=c_kv_a_norm_g, c_w_kv_b=c_w_kv_b,
             c_q_norm_g=c_q_norm_g, c_k_norm_g=c_k_norm_g,
             w_branch_a=w_branch_a, w_branch_b=w_branch_b, w_branch_c=w_branch_c,
             w_out=w_out, mlp_norm_g=mlp_norm_g, w_up=w_up, w_down=w_down)
    return (_trunk(x_prompt, meta_tokens, p), _trunk(x_sample, meta_tokens, p))
```

```python
import functools
import math

import numpy as np
import jax
import jax.numpy as jnp
from jax import lax
from jax.experimental import pallas as pl
from jax.experimental.pallas import tpu as pltpu

F32 = jnp.float32
BF16 = jnp.bfloat16

D_MODEL = 1024
DEPTH = 2
N_META = 16
GRID_W = 64
ROPE_THETA = 10000.0
NORM_EPS = 1e-6
A_HEADS, A_KV_HEADS, A_HEAD_DIM = 8, 2, 64
B_HEADS, B_HEAD_DIM = 4, 64
B_V_DIM = 2 * B_HEAD_DIM
C_HEADS, C_NOPE, C_ROPE, C_V = 4, 128, 64, 128
C_Q_LORA, C_KV_LORA = 256, 128
N_BRANCH = 3
D_FF = 4 * D_MODEL
IN_SPLITS = (
    A_HEADS * A_HEAD_DIM, A_KV_HEADS * A_HEAD_DIM, A_KV_HEADS * A_HEAD_DIM,
    2 * B_HEADS * B_HEAD_DIM, 2 * B_HEADS * B_HEAD_DIM, B_HEADS * B_V_DIM,
    C_Q_LORA, C_KV_LORA, C_ROPE,
    N_BRANCH * D_MODEL,
)
IN_COLS = sum(IN_SPLITS)

LOG2E = math.log2(math.e)
SUB = 256
N_STREAMS = 4
N_STREAMS_ALIBI = 2
TK = 512
TAIL = TK
ROW_TILE = 512
ONES_ROWS = 16
MASK_SCORE = -30000.0
EXP_ZERO = -110.0
QK_BOUND_SLACK = 1.02
VMEM_LIMIT = 52 * 1024 * 1024


def _cparams(sem):
    return pltpu.CompilerParams(dimension_semantics=sem, vmem_limit_bytes=VMEM_LIMIT)


def _norm_matmul_kernel(x_ref, g_ref, w_ref, o_ref):
    x = x_ref[...]
    r = lax.rsqrt(jnp.mean(x * x, axis=-1, keepdims=True) + NORM_EPS)
    h = (x * r * g_ref[...]).astype(BF16)
    o_ref[...] = jnp.dot(h, w_ref[...], preferred_element_type=F32)


def norm_matmul(x, g, w, tn):
    R, K = x.shape
    C = w.shape[1]
    return pl.pallas_call(
        _norm_matmul_kernel,
        out_shape=jax.ShapeDtypeStruct((R, C), F32),
        grid=(R // ROW_TILE, C // tn),
        in_specs=[
            pl.BlockSpec((ROW_TILE, K), lambda i, j: (i, 0)),
            pl.BlockSpec((1, K), lambda i, j: (0, 0)),
            pl.BlockSpec((K, tn), lambda i, j: (0, j)),
        ],
        out_specs=pl.BlockSpec((ROW_TILE, tn), lambda i, j: (i, j)),
        compiler_params=_cparams(("parallel", "arbitrary")),
        name="norm_matmul",
    )(x, g.reshape(1, K), w)


def _dot_tn(a, w):
    return lax.dot_general(a.astype(BF16), w, (((0,), (0,)), ((), ())),
                           preferred_element_type=F32)


def _merge_kernel(lam_ref, oa_ref, ob_ref, oc_ref, gate_ref, bg_ref, gsub_ref, x_ref,
                  wa_ref, wb_ref, wc_ref, wo_ref, out_ref, *, n_valid, lam_init):
    pa = _dot_tn(oa_ref[...], wa_ref[...])
    heads = []
    for h in range(B_HEADS):
        o1 = ob_ref[(2 * h) * B_V_DIM:(2 * h + 1) * B_V_DIM, :]
        o2 = ob_ref[(2 * h + 1) * B_V_DIM:(2 * h + 2) * B_V_DIM, :]
        d = o1 - lam_ref[0] * o2
        r = lax.rsqrt(jnp.mean(d * d, axis=0, keepdims=True) + NORM_EPS)
        heads.append(d * r * gsub_ref[...] * (1.0 - lam_init))
    pb = _dot_tn(jnp.concatenate(heads, axis=0), wb_ref[...])
    pc = _dot_tn(oc_ref[...], wc_ref[...])
    g = jax.nn.sigmoid(gate_ref[...] + bg_ref[...])
    merged = (g[:, :D_MODEL] * pa + g[:, D_MODEL:2 * D_MODEL] * pb
              + g[:, 2 * D_MODEL:] * pc)
    y = x_ref[...] + jnp.dot(merged.astype(BF16), wo_ref[...], preferred_element_type=F32)
    rows = pl.program_id(1) * ROW_TILE + lax.broadcasted_iota(jnp.int32, (ROW_TILE, 1), 0)
    out_ref[...] = jnp.where(rows < n_valid, y, 0.0)


def merge_branches(lam, oa, ob, oc, z, b_gate, g_sub, x, wa, wb, wc, wo, *, n_valid, lam_init):
    B, N, _ = x.shape
    col = lambda a: pl.BlockSpec((None, a.shape[1], ROW_TILE), lambda b, i: (b, 0, i))
    row = lambda c: pl.BlockSpec((None, ROW_TILE, c), lambda b, i: (b, i, 0))
    full = lambda a: pl.BlockSpec(a.shape, lambda b, i: (0, 0))
    return pl.pallas_call(
        functools.partial(_merge_kernel, n_valid=n_valid, lam_init=lam_init),
        out_shape=jax.ShapeDtypeStruct((B, N, D_MODEL), F32),
        grid=(B, N // ROW_TILE),
        in_specs=[pl.BlockSpec(memory_space=pltpu.SMEM), col(oa), col(ob), col(oc),
                  row(N_BRANCH * D_MODEL),
                  pl.BlockSpec((1, N_BRANCH * D_MODEL), lambda b, i: (0, 0)),
                  pl.BlockSpec((B_V_DIM, 1), lambda b, i: (0, 0)),
                  row(D_MODEL), full(wa), full(wb), full(wc), full(wo)],
        out_specs=row(D_MODEL),
        compiler_params=_cparams(("parallel", "parallel")),
        name="merge_branches",
    )(lam.reshape(1), oa, ob, oc, z, b_gate.reshape(1, -1), g_sub.reshape(-1, 1), x,
      wa, wb, wc, wo)


def _mlp_kernel(x_ref, g_ref, wu_ref, wd_ref, out_ref, *, ff_chunk):
    x = x_ref[...]
    r = lax.rsqrt(jnp.mean(x * x, axis=-1, keepdims=True) + NORM_EPS)
    h = (x * r * g_ref[...]).astype(BF16)
    acc = x
    for c in range(D_FF // ff_chunk):
        u = jnp.dot(h, wu_ref[:, c * ff_chunk:(c + 1) * ff_chunk],
                    preferred_element_type=F32)
        a = jnp.square(jnp.maximum(u, 0.0)).astype(BF16)
        acc = acc + jnp.dot(a, wd_ref[c * ff_chunk:(c + 1) * ff_chunk, :],
                            preferred_element_type=F32)
    out_ref[...] = acc


def mlp_block(x, g, wu, wd):
    R = x.shape[0]
    return pl.pallas_call(
        functools.partial(_mlp_kernel, ff_chunk=1024),
        out_shape=jax.ShapeDtypeStruct((R, D_MODEL), F32),
        grid=(R // ROW_TILE,),
        in_specs=[
            pl.BlockSpec((ROW_TILE, D_MODEL), lambda i: (i, 0)),
            pl.BlockSpec((1, D_MODEL), lambda i: (0, 0)),
            pl.BlockSpec(wu.shape, lambda i: (0, 0)),
            pl.BlockSpec(wd.shape, lambda i: (0, 0)),
        ],
        out_specs=pl.BlockSpec((ROW_TILE, D_MODEL), lambda i: (i, 0)),
        compiler_params=_cparams(("parallel",)),
        name="mlp_block",
    )(x, g.reshape(1, -1), wu, wd)


def _dot_nt(a, b):
    return lax.dot_general(a, b, (((1,), (1,)), ((), ())), preferred_element_type=F32)


def _attn_kernel(*refs, n_chunks, dv, feat0, alibi, meta_tile, n_streams):
    if alibi:
        sigma_ref, reach_ref, q_ref, kx_ref, vt_ref = refs[:5]
    else:
        q_ref, kx_ref, vt_ref = refs[:3]
    o_ref = refs[-8]
    m_ref, acc_ref, s_ring, bm_ring, p_ring, al_ring, q2_ref = refs[-7:]
    tqt = n_streams * SUB
    qi = pl.program_id(2)
    streams = range(n_streams)

    m_ref[...] = jnp.full(m_ref.shape, -1e30, F32)
    acc_ref[...] = jnp.zeros(acc_ref.shape, F32)

    ex = jnp.exp if alibi else jnp.exp2

    def q_form(st, form):
        if alibi:
            return q2_ref[form, st]
        return q_ref[st * SUB:(st + 1) * SUB, :]

    if alibi:
        for st in streams:
            q = q_ref[st * SUB:(st + 1) * SUB, :]
            lane = lax.broadcasted_iota(jnp.int32, q.shape, 1)
            q2_ref[0, st] = q
            q2_ref[1, st] = jnp.where((lane >= feat0) & (lane < feat0 + 4), -q, q)
        sigma = sigma_ref[pl.program_id(1)]
        if meta_tile:
            n_diag, j_before = 1, jnp.int32(0)
        else:
            n_diag, j_before = max(1, tqt // TK), (qi * tqt) // TK
    else:
        n_diag = 0
    n_pure = (n_chunks if meta_tile else n_chunks + 1 - n_diag) if alibi else n_chunks + 1
    if alibi:
        reach = reach_ref[pl.program_id(1)]
        if meta_tile:
            lo, hi = jnp.int32(0), jnp.minimum(n_pure, reach + 1)
        else:
            lo = jnp.maximum(0, j_before - reach)
            hi = jnp.minimum(n_pure, j_before + reach + 2)
        odd = (hi - lo) % 2
        grow_lo = (odd == 1) & (lo > 0)
        lo = lo - jnp.where(grow_lo, 1, 0)
        hi = hi + jnp.where((odd == 1) & jnp.logical_not(grow_lo), 1, 0)
        cnt = hi - lo
        extras = [n_chunks] if meta_tile else [j_before + d for d in range(n_diag)]
    else:
        lo, cnt = 0, n_pure - n_pure % 2
        extras = [n_chunks] if n_pure % 2 else []
    assert n_pure % 2 == 0 or not alibi

    def base_i(st):
        return jnp.int32(0) if meta_tile else N_META + qi * tqt + st * SUB

    def delta(st, b):
        return (base_i(st) - jnp.where(b == n_chunks, 0, N_META + b * TK)).astype(F32)

    def chunk_rows(b):
        if isinstance(b, int):
            return pl.ds(b * TK, TK)
        return pl.ds(pl.multiple_of(b * TK, TK), TK)

    def block(pos, extra=None):
        if extra is not None:
            return extras[extra], None, alibi
        v = lo + pos
        if not alibi:
            return v, None, False
        if meta_tile:
            return v, True, False
        after = v > j_before
        b = jnp.where(v == 0, n_chunks, v - 1 + jnp.where(after, n_diag, 0))
        return b, after, False

    def stage_a(blk, slot):
        b, after, diag = blk
        kx = kx_ref[chunk_rows(b), :]
        for st in streams:
            if diag:
                c = sigma * delta(st, b)
                s = jnp.minimum(_dot_nt(kx, q_form(st, 0)) - c, _dot_nt(kx, q_form(st, 1)) + c)
            elif alibi:
                s = _dot_nt(kx, q_form(st, 1 if after is True else after.astype(jnp.int32)))
            else:
                s = _dot_nt(kx, q_form(st, 0))
            s_ring[slot, st] = s
            bm_ring[slot, st] = jnp.max(s, axis=0, keepdims=True)

    def stage_b(blk, slot):
        b, after, diag = blk
        for st in streams:
            shift = 0.0
            if alibi and not diag:
                d = sigma * delta(st, b)
                shift = jnp.where(after, d, -d)
            m_old = m_ref[st]
            m_new = jnp.maximum(m_old, bm_ring[slot, st] + shift)
            al_ring[slot, st] = ex(m_old - m_new)
            p_ring[slot, st] = ex(s_ring[slot, st] - (m_new - shift)).astype(BF16)
            m_ref[st] = m_new

    def stage_c(blk, slot):
        b, _, _ = blk
        vt = vt_ref[:, chunk_rows(b)]
        for st in streams:
            acc_ref[st] = acc_ref[st] * al_ring[slot, st] + jnp.dot(
                vt, p_ring[slot, st], preferred_element_type=F32)

    stage_a(block(0), 0)
    stage_a(block(1), 1)
    stage_b(block(0), 0)

    def pair(tp, carry):
        w = 2 * tp
        stage_a(block(w + 2), 0)
        stage_b(block(w + 1), 1)
        stage_c(block(w), 0)
        stage_a(block(w + 3), 1)
        stage_b(block(w + 2), 0)
        stage_c(block(w + 1), 1)
        return carry

    lax.fori_loop(0, cnt // 2 - 1, pair, 0)

    def at(k):
        return block(cnt + k) if k < 0 else block(None, extra=k)

    for k in range(len(extras) + 2):
        if k < len(extras):
            stage_a(at(k), k % 2)
        if k - 1 < len(extras):
            stage_b(at(k - 1), (k - 1) % 2)
        stage_c(at(k - 2), k % 2)

    for st in streams:
        acc = acc_ref[st]
        o_ref[:, st * SUB:(st + 1) * SUB] = acc[:dv, :] / acc[dv:dv + 1, :]


def _flash_call(q, kx, vt, sigma, *, heads, dv, feat0, meta_tile, n_streams, q0, n_q, prev=None):
    B, N, _ = q.shape
    H, Kd = heads, q.shape[2] // heads
    G, Gv, dvx = kx.shape[2] // Kd, vt.shape[1], vt.shape[2]
    n_chunks = (N - TAIL) // TK
    assert n_chunks * TK + TAIL == N
    tqt = n_streams * SUB
    assert n_q % tqt == 0 and q0 % tqt == 0
    t0 = q0 // tqt
    alibi = sigma is not None
    in_specs = [
        pl.BlockSpec((None, tqt, Kd), lambda b, h, i: (b, t0 + i, h)),
        pl.BlockSpec((None, N, Kd), lambda b, h, i: (b, 0, h // (H // G))),
        pl.BlockSpec((None, None, dvx, N), lambda b, h, i: (b, h // (H // Gv), 0, 0)),
    ]
    args = [q, kx, vt]
    if alibi:
        in_specs = [pl.BlockSpec(memory_space=pltpu.SMEM)] * 2 + in_specs
        args = list(sigma) + args
    aliases = {}
    if prev is not None:
        aliases = {len(args): 0}
        in_specs = in_specs + [pl.BlockSpec(memory_space=pl.ANY)]
        args = args + [prev]
    return pl.pallas_call(
        functools.partial(_attn_kernel, n_chunks=n_chunks, dv=dv, feat0=feat0, alibi=alibi,
                          meta_tile=meta_tile, n_streams=n_streams),
        out_shape=jax.ShapeDtypeStruct((B, H, dv, N), F32),
        grid=(B, H, n_q // tqt),
        in_specs=in_specs,
        out_specs=pl.BlockSpec((None, None, dv, tqt), lambda b, h, i: (b, h, 0, t0 + i)),
        input_output_aliases=aliases,
        scratch_shapes=[pltpu.VMEM((n_streams, 1, SUB), F32),
                        pltpu.VMEM((n_streams, dvx, SUB), F32),
                        pltpu.VMEM((2, n_streams, TK, SUB), F32),
                        pltpu.VMEM((2, n_streams, 1, SUB), F32),
                        pltpu.VMEM((2, n_streams, TK, SUB), BF16),
                        pltpu.VMEM((2, n_streams, 1, SUB), F32),
                        pltpu.VMEM((2, n_streams, SUB, Kd), BF16)],
        compiler_params=_cparams(("parallel", "parallel", "arbitrary")),
        name=("flash_alibi" if alibi else "flash_plain") + ("_meta" if meta_tile else ""),
    )(*args)


def flash_attention(q, kx, vt, *, heads, dv, feat0, sigma=None):
    N = q.shape[1]
    n = N - TAIL
    kw = dict(heads=heads, dv=dv, feat0=feat0)
    n_streams = N_STREAMS if sigma is None else N_STREAMS_ALIBI
    o = _flash_call(q, kx, vt, sigma, meta_tile=False, n_streams=n_streams, q0=0, n_q=n, **kw)
    o = _flash_call(q, kx, vt, sigma, meta_tile=True, n_streams=1, q0=n, n_q=SUB, prev=o, **kw)
    return o.reshape(o.shape[0], -1, N)


Z_GATE = N_BRANCH * D_MODEL
Z_HALF = 3584
_ZQA, _ZQAR, _ZKA, _ZKAR, _ZVA = 0, 512, 1024, 1152, 1280
_ZQB, _ZKB, _ZVB = 1408, 1920, 2432
_ZCQ, _ZCKV, _ZKPE = 2944, 3200, 3328
DQK_C = C_NOPE + C_ROPE


def _rot_perm(d, block):
    j = np.arange(d)
    half = block // 2
    lower = (j % block) < half
    return np.where(lower, j + half, j - half), np.where(lower, -1.0, 1.0)


def _block_ones(n, width):
    g = np.arange(n) // width
    return jnp.asarray(g[:, None] == g[None, :], BF16)


def _placement(n_in, n_out, src_width, dst_stride, dst_off=0):
    p = np.zeros((n_in, n_out), np.float32)
    j = np.arange(n_in)
    p[j, (j // src_width) * dst_stride + dst_off + j % src_width] = 1.0
    return jnp.asarray(p, BF16)


def _group_sum(x2, ones):
    hi = x2.astype(BF16)
    lo = (x2 - hi.astype(F32)).astype(BF16)
    return (jnp.dot(hi, ones, preferred_element_type=F32)
            + jnp.dot(lo, ones, preferred_element_type=F32))


def _place(x_bf16, p):
    return jnp.dot(x_bf16, p, preferred_element_type=F32)


def _tile_lanes(x, reps):
    return jnp.concatenate([x] * reps, axis=1)


def _prep_kernel(z_ref, tab_ref, qfb_ref, kfb_ref, gains_ref, gcq_ref, gckv_ref, wqb_ref, wkvb_ref,
                 s512_ref, s128_ref, s128x512_ref, e_pn_ref, e_np_ref, s64x256_ref,
                 pq_ref, pk_ref, pva_ref, ppe_ref,
                 qa_ref, kxa_ref, vta_ref, qb_ref, kxb_ref, vtb_ref, qc_ref, kxc_ref, vtc_ref,
                 *, n_valid):
    tm = z_ref.shape[0]
    rows = pl.program_id(1) * tm + lax.broadcasted_iota(jnp.int32, (tm, 1), 0)
    valid = (rows < n_valid).astype(F32)
    lane128 = lax.broadcasted_iota(jnp.int32, (1, 128), 1)
    flag_col = (lane128 == 64).astype(F32)
    one_col0 = (lane128 == 0).astype(F32)
    g = lambda i, w: gains_ref[i:i + 1, :w]
    cos_a, sin_a = tab_ref[:, 0:128], tab_ref[:, 128:256]
    cos_c, sin_c = tab_ref[:, 256:384], tab_ref[:, 384:512]
    cs_k = tab_ref[:, 512:640]
    rs = lambda ms: lax.rsqrt(ms + NORM_EPS)

    def v_transposed(v_ext, rows_out):
        return (v_ext * valid).T[:rows_out].astype(BF16)

    q, qr = z_ref[:, _ZQA:_ZQA + 512], z_ref[:, _ZQAR:_ZQAR + 512]
    r = rs(_group_sum(q * q, s512_ref[...]) * (1.0 / A_HEAD_DIM))
    qo = ((q * g(0, 512) * _tile_lanes(cos_a, 4) + qr * g(1, 512) * _tile_lanes(sin_a, 4))
          * (r * (A_HEAD_DIM ** -0.5 * LOG2E)))
    mask_row = MASK_SCORE * _tile_lanes(flag_col, 8)
    qa_ref[...] = (_place(qo.astype(BF16), pq_ref[...]) + mask_row).astype(BF16)
    k, kr = z_ref[:, _ZKA:_ZKA + 128], z_ref[:, _ZKAR:_ZKAR + 128]
    r = rs(_group_sum(k * k, s128_ref[...]) * (1.0 / A_HEAD_DIM))
    ko = (k * g(2, 128) * cos_a + kr * g(3, 128) * sin_a) * (r * valid)
    kxa_ref[...] = (_place(ko.astype(BF16), pk_ref[...])
                    + (1.0 - valid) * _tile_lanes(flag_col, 2)).astype(BF16)
    v = z_ref[:, _ZVA:_ZVA + 128].astype(BF16)
    for h in range(A_KV_HEADS):
        v_ext = _place(v, pva_ref[h]) + flag_col
        vta_ref[h] = v_transposed(v_ext, A_HEAD_DIM + ONES_ROWS)

    q = z_ref[:, _ZQB:_ZQB + 512]
    r = rs(_group_sum(q * q, s512_ref[...]) * (1.0 / B_HEAD_DIM))
    qo = q * g(4, 512) * (r * (B_HEAD_DIM ** -0.5))
    qb_ref[...] = (_place(qo.astype(BF16), pq_ref[...]) + qfb_ref[...]).astype(BF16)
    k = z_ref[:, _ZKB:_ZKB + 512]
    r = rs(_group_sum(k * k, s512_ref[...]) * (1.0 / B_HEAD_DIM))
    ko = k * g(5, 512) * (r * valid)
    kxb_ref[...] = (_place(ko.astype(BF16), pq_ref[...]) + kfb_ref[...] * valid
                    + (1.0 - valid) * _tile_lanes((lane128 == 68).astype(F32), 8)).astype(BF16)
    for h in range(B_HEADS):
        v_h = z_ref[:, _ZVB + h * B_V_DIM:_ZVB + (h + 1) * B_V_DIM]
        v_ext = jnp.concatenate([v_h, jnp.broadcast_to(one_col0, (tm, 128))], axis=1)
        vtb_ref[h] = v_transposed(v_ext, B_V_DIM + ONES_ROWS)

    cq = z_ref[:, _ZCQ:_ZCQ + C_Q_LORA]
    cqn = cq * rs(jnp.mean(cq * cq, axis=-1, keepdims=True)) * gcq_ref[...]
    q2 = jnp.dot(cqn.astype(BF16), wqb_ref[...], preferred_element_type=F32)
    nope, pe, per = q2[:, :512], q2[:, 512:768], q2[:, 768:1024]
    nn, pp = nope * nope, pe * pe
    r_n = rs((_group_sum(nn, s128x512_ref[...]) + _group_sum(pp, e_pn_ref[...])) * (1.0 / DQK_C))
    r_p = rs((_group_sum(pp, s64x256_ref[...]) + _group_sum(nn, e_np_ref[...])) * (1.0 / DQK_C))
    sc = DQK_C ** -0.5 * LOG2E
    q_nope = (nope * g(6, 512) * (r_n * sc)).astype(BF16)
    q_pe = ((pe * g(7, 256) * _tile_lanes(cos_c, 2) + per * g(8, 256) * _tile_lanes(sin_c, 2))
            * (r_p * sc))
    pe_slots = (_place(q_pe.astype(BF16), ppe_ref[...])
                + MASK_SCORE * _tile_lanes(flag_col, 4)).astype(BF16)
    qc_ref[...] = jnp.concatenate(
        [t for h in range(C_HEADS)
         for t in (q_nope[:, 128 * h:128 * (h + 1)], pe_slots[:, 128 * h:128 * (h + 1)])], axis=1)

    ckv = z_ref[:, _ZCKV:_ZCKV + C_KV_LORA]
    ckvn = ckv * rs(jnp.mean(ckv * ckv, axis=-1, keepdims=True)) * gckv_ref[...]
    kv = jnp.dot(ckvn.astype(BF16), wkvb_ref[...], preferred_element_type=F32)
    k_nope = kv[:, :512]
    kk = z_ref[:, _ZKPE:_ZKPE + 128]
    pe_ss = jnp.sum(kk[:, :64] * kk[:, :64], axis=-1, keepdims=True)
    r_k = rs((_group_sum(k_nope * k_nope, s128x512_ref[...]) + pe_ss) * (1.0 / DQK_C))
    k_no = (k_nope * g(9, 512) * (r_k * valid)).astype(BF16)
    t = kk * g(10, 128) * cs_k
    k_pe = jnp.where(lane128 < 64, t + pltpu.roll(t, 64, axis=1), 0.0)
    pad_flag = ((1.0 - valid) * flag_col).astype(BF16)
    kxc_ref[...] = jnp.concatenate(
        [t2 for h in range(C_HEADS)
         for t2 in (k_no[:, 128 * h:128 * (h + 1)],
                    (k_pe * r_k[:, 128 * h:128 * h + 1] * valid).astype(BF16) + pad_flag)], axis=1)
    for h in range(C_HEADS):
        v_h = kv[:, 512 + h * C_V:512 + (h + 1) * C_V]
        v_ext = jnp.concatenate([v_h, jnp.broadcast_to(one_col0, (tm, 128))], axis=1)
        vtc_ref[h] = v_transposed(v_ext, C_V + ONES_ROWS)


def _positions(n):
    pad = TAIL - N_META
    rows = n // GRID_W
    row = np.concatenate([np.repeat(np.arange(rows), GRID_W), np.full(N_META, -1.0), np.zeros(pad)])
    col = np.concatenate([np.tile(np.arange(GRID_W), rows), np.arange(N_META), np.zeros(pad)])
    lin = np.concatenate([N_META + np.arange(n), np.arange(N_META), np.zeros(pad)])
    f = lambda a: jnp.asarray(a, F32)
    return f(row), f(col), f(lin)


def _rope_tables(n):
    row, col, lin = _positions(n)

    def angles(pos, d):
        inv = ROPE_THETA ** (-2.0 * jnp.arange(d // 2, dtype=F32) / d)
        ang = pos[:, None] * inv[None, :]
        return jnp.concatenate([ang, ang], axis=1)

    ang_a = jnp.concatenate([angles(row, A_HEAD_DIM // 2), angles(col, A_HEAD_DIM // 2)], axis=1)
    ang_c = angles(lin, C_ROPE)
    two = lambda t: jnp.concatenate([t, t], axis=1)
    return jnp.concatenate([two(jnp.cos(ang_a)), two(jnp.sin(ang_a)), two(jnp.cos(ang_c)),
                            two(jnp.sin(ang_c)), jnp.cos(ang_c), jnp.sin(ang_c)], axis=1)


def _alibi_reach(sigma, qk_bound):
    dist = (2.0 * QK_BOUND_SLACK * qk_bound - EXP_ZERO) / sigma
    return jnp.clip(jnp.floor(dist / TK), 0, 1 << 20).astype(jnp.int32)


def _alibi_tables():
    slopes = 2.0 ** (-8.0 * np.arange(1, B_HEADS + 1) / B_HEADS)
    sig = np.repeat(slopes, 2)
    idx = np.arange(ROW_TILE)
    ii, jj = idx % SUB, idx % TK
    qf = np.zeros((ROW_TILE, 2 * B_HEADS, 128), np.float32)
    kf = np.zeros((ROW_TILE, 2 * B_HEADS, 128), np.float32)
    ii_hi = (ii // 256) * 256
    qf[:, :, 64] = -ii_hi[:, None] * sig[None, :]
    qf[:, :, 65] = -(ii - ii_hi)[:, None] * sig[None, :]
    qf[:, :, 66] = 256.0 * sig[None, :]
    qf[:, :, 67] = sig[None, :]
    qf[:, :, 68] = MASK_SCORE
    kf[:, :, 64] = 1.0
    kf[:, :, 65] = 1.0
    kf[:, :, 66] = (jj // 256)[:, None]
    kf[:, :, 67] = (jj % 256)[:, None]
    flat = lambda a: jnp.asarray(a.reshape(ROW_TILE, -1), F32)
    return jnp.asarray(sig, F32), flat(qf), flat(kf)


def _prep_params(l, p):
    w = p['w_in'][l]
    cuts = np.concatenate([[0], np.cumsum(IN_SPLITS)])
    sec = lambda i: w[:, int(cuts[i]):int(cuts[i + 1])]
    qa, ka, va, qb, kb, vb, cq, ckv, kpe, gates = [sec(i) for i in range(10)]
    perm_a, sign_a = _rot_perm(A_HEAD_DIM, A_HEAD_DIM // 2)
    perm_c, sign_c = _rot_perm(C_ROPE, C_ROPE)

    def rot_cols(wm, perm, sign, heads):
        d = len(perm)
        cols = (np.arange(heads)[:, None] * d + perm[None, :]).reshape(-1)
        return wm[:, cols] * jnp.asarray(np.tile(sign, heads), F32)

    zero = lambda c: jnp.zeros((D_MODEL, c), F32)
    second = jnp.concatenate(
        [qa, rot_cols(qa, perm_a, sign_a, A_HEADS), ka, rot_cols(ka, perm_a, sign_a, A_KV_HEADS), va,
         qb, kb, vb, cq, ckv, kpe, rot_cols(kpe, perm_c, sign_c, 1)], axis=1)
    w_in = jnp.concatenate([gates, zero(Z_HALF - Z_GATE), second,
                            zero(Z_HALF - second.shape[1])], axis=1).astype(BF16)

    wq = p['c_w_q_b'][l].reshape(C_Q_LORA, C_HEADS, DQK_C)
    q_nope = wq[:, :, :C_NOPE].reshape(C_Q_LORA, -1)
    q_pe = wq[:, :, C_NOPE:].reshape(C_Q_LORA, -1)
    w_qb = jnp.concatenate([q_nope, q_pe, rot_cols(q_pe, perm_c, sign_c, C_HEADS)], 1).astype(BF16)
    wkv = p['c_w_kv_b'][l].reshape(C_KV_LORA, C_HEADS, C_NOPE + C_V)
    w_kvb = jnp.concatenate([wkv[:, :, :C_NOPE].reshape(C_KV_LORA, -1),
                             wkv[:, :, C_NOPE:].reshape(C_KV_LORA, -1)], 1).astype(BF16)

    tile = lambda v, r: jnp.tile(v, r)
    gaq, gak = p['a_q_norm_g'][l], p['a_k_norm_g'][l]
    gcq, gck = p['c_q_norm_g'][l], p['c_k_norm_g'][l]
    rows = [tile(gaq, 8), tile(gaq[perm_a], 8), tile(gak, 2), tile(gak[perm_a], 2),
            tile(p['b_q_norm_g'][l], 8), tile(p['b_k_norm_g'][l], 8),
            tile(gcq[:C_NOPE], 4), tile(gcq[C_NOPE:], 4), tile(gcq[C_NOPE:][perm_c], 4),
            tile(gck[:C_NOPE], 4), jnp.concatenate([gck[C_NOPE:], gck[C_NOPE:][perm_c]])]
    gains = jnp.stack([jnp.pad(r, (0, 512 - r.shape[0])) for r in rows]
                      + [jnp.zeros((512,), F32)] * 5)
    pva = jnp.stack([_placement(128, 128, 64, 0) * (np.arange(128)[:, None] // 64 == h)
                     for h in range(A_KV_HEADS)]).astype(BF16)
    consts = (_block_ones(512, 64), _block_ones(128, 64), _block_ones(512, 128),
              jnp.asarray(np.arange(256)[:, None] // 64 == np.arange(512)[None, :] // 128, BF16),
              jnp.asarray(np.arange(512)[:, None] // 128 == np.arange(256)[None, :] // 64, BF16),
              _block_ones(256, 64),
              _placement(512, 1024, 64, 128), _placement(128, 256, 64, 128), pva,
              _placement(256, 512, 64, 128))
    return w_in, w_qb, w_kvb, gains, consts


def prep_attention_inputs(z, tabs, qfb, kfb, gains, g_cq, g_ckv, w_qb, w_kvb, consts, *, n_valid):
    B, N, _ = z.shape
    tm = ROW_TILE
    row = lambda c: pl.BlockSpec((None, tm, c), lambda b, i: (b, i, 0))
    const = lambda a: pl.BlockSpec(a.shape, lambda b, i: (0,) * a.ndim)
    vt_spec = lambda g, d: pl.BlockSpec((None, g, d, tm), lambda b, i: (b, 0, 0, i))
    vt_shape = lambda g, d: jax.ShapeDtypeStruct((B, g, d, N), BF16)
    tok = lambda c: jax.ShapeDtypeStruct((B, N, c), BF16)
    small = [gains, g_cq.reshape(1, -1), g_ckv.reshape(1, -1), w_qb, w_kvb, *consts]
    return pl.pallas_call(
        functools.partial(_prep_kernel, n_valid=n_valid),
        out_shape=[tok(1024), tok(256), vt_shape(A_KV_HEADS, A_HEAD_DIM + ONES_ROWS),
                   tok(1024), tok(1024), vt_shape(B_HEADS, B_V_DIM + ONES_ROWS),
                   tok(1024), tok(1024), vt_shape(C_HEADS, C_V + ONES_ROWS)],
        grid=(B, N // tm),
        in_specs=[pl.BlockSpec((None, tm, Z_HALF), lambda b, i: (b, i, 1)),
                  pl.BlockSpec((tm, tabs.shape[1]), lambda b, i: (i, 0)),
                  const(qfb), const(kfb)] + [const(a) for a in small],
        out_specs=[row(1024), row(256), vt_spec(A_KV_HEADS, A_HEAD_DIM + ONES_ROWS),
                   row(1024), row(1024), vt_spec(B_HEADS, B_V_DIM + ONES_ROWS),
                   row(1024), row(1024), vt_spec(C_HEADS, C_V + ONES_ROWS)],
        compiler_params=_cparams(("parallel", "parallel")),
        name="prep_attention_inputs",
    )(z, tabs, qfb, kfb, *small)


def _layer(x, tabs, l, p):
    B, N, _ = x.shape
    n_valid = N - TAIL + N_META
    w_in, w_qb, w_kvb, gains, consts = _prep_params(l, p)
    z = norm_matmul(x.reshape(B * N, D_MODEL), p['attn_norm_g'][l], w_in, tn=2 * Z_HALF // 4)
    z = z.reshape(B, N, 2 * Z_HALF)
    sig, qfb, kfb = _alibi_tables()
    qa, kxa, vta, qb, kxb, vtb, qc, kxc, vtc = prep_attention_inputs(
        z, tabs, qfb, kfb, gains, p['c_q_a_norm_g'][l], p['c_kv_a_norm_g'][l], w_qb, w_kvb,
        consts, n_valid=n_valid)
    reach = _alibi_reach(sig, B_HEAD_DIM ** 0.5 * jnp.max(jnp.abs(p['b_q_norm_g'][l]))
                         * jnp.max(jnp.abs(p['b_k_norm_g'][l])))
    oa = flash_attention(qa, kxa, vta, heads=A_HEADS, dv=A_HEAD_DIM, feat0=A_HEAD_DIM)
    ob = flash_attention(qb, kxb, vtb, heads=2 * B_HEADS, dv=B_V_DIM, feat0=B_HEAD_DIM,
                         sigma=(sig, reach))
    oc = flash_attention(qc, kxc, vtc, heads=C_HEADS, dv=C_V, feat0=DQK_C)

    lam_init = 0.8 - 0.6 * math.exp(-0.3 * l)
    lam = (jnp.exp(jnp.sum(p['b_lambda_q1'][l] * p['b_lambda_k1'][l]))
           - jnp.exp(jnp.sum(p['b_lambda_q2'][l] * p['b_lambda_k2'][l])) + lam_init)
    x = merge_branches(lam, oa, ob, oc, z, p['b_gate'][l], p['b_subln_g'][l], x,
                       p['w_branch_a'][l].astype(BF16), p['w_branch_b'][l].astype(BF16),
                       p['w_branch_c'][l].astype(BF16), p['w_out'][l].astype(BF16),
                       n_valid=n_valid, lam_init=lam_init)
    x2 = mlp_block(x.reshape(B * N, D_MODEL), p['mlp_norm_g'][l], p['w_up'][l].astype(BF16),
                   p['w_down'][l].astype(BF16))
    return x2.reshape(B, N, D_MODEL)


def _trunk(x, meta_tokens, p):
    B, n, _ = x.shape
    assert n % TK == 0 and n % (N_STREAMS * SUB) == 0 and n % GRID_W == 0
    tabs = _rope_tables(n)
    meta = jnp.broadcast_to(meta_tokens[None], (B, N_META, D_MODEL))
    h = jnp.concatenate([x, meta, jnp.zeros((B, TAIL - N_META, D_MODEL), F32)], axis=1)
    for l in range(DEPTH):
        h = _layer(h, tabs, l, p)
    return h[:, :n]


def kernel(x_prompt, x_sample, meta_tokens, attn_norm_g, w_in, b_gate, a_q_norm_g, a_k_norm_g, b_q_norm_g, b_k_norm_g, b_lambda_q1, b_lambda_k1, b_lambda_q2, b_lambda_k2, b_subln_g, c_q_a_norm_g, c_w_q_b, c_kv_a_norm_g, c_w_kv_b, c_q_norm_g, c_k_norm_g, w_branch_a, w_branch_b, w_branch_c, w_out, mlp_norm_g, w_up, w_down):
    p = dict(attn_norm_g=attn_norm_g, w_in=w_in, b_gate=b_gate,
             a_q_norm_g=a_q_norm_g, a_k_norm_g=a_k_norm_g,
             b_q_norm_g=b_q_norm_g, b_k_norm_g=b_k_norm_g,
             b_lambda_q1=b_lambda_q1, b_lambda_k1=b_lambda_k1,
             b_lambda_q2=b_lambda_q2, b_lambda_k2=b_lambda_k2, b_subln_g=b_subln_g,
             c_q_a_norm_g=c_q_a_norm_g, c_w_q_b=c_w_q_b,
             c_kv_a_norm_g=c_kv_a_norm_g, c_w_kv_b=c_w_kv_b,
             c_q_norm_g=c_q_norm_g, c_k_norm_g=c_k_norm_g,
             w_branch_a=w_branch_a, w_branch_b=w_branch_b, w_branch_c=w_branch_c,
             w_out=w_out, mlp_norm_g=mlp_norm_g, w_up=w_up, w_down=w_down)
    return (_trunk(x_prompt, meta_tokens, p), _trunk(x_sample, meta_tokens, p))
```

```python
import functools
import math

import numpy as np
import jax
import jax.numpy as jnp
from jax import lax
from jax.experimental import pallas as pl
from jax.experimental.pallas import tpu as pltpu

F32 = jnp.float32
BF16 = jnp.bfloat16

D_MODEL = 1024
DEPTH = 2
N_META = 16
GRID_W = 64
ROPE_THETA = 10000.0
NORM_EPS = 1e-6
A_HEADS, A_KV_HEADS, A_HEAD_DIM = 8, 2, 64
B_HEADS, B_HEAD_DIM = 4, 64
B_V_DIM = 2 * B_HEAD_DIM
C_HEADS, C_NOPE, C_ROPE, C_V = 4, 128, 64, 128
C_Q_LORA, C_KV_LORA = 256, 128
N_BRANCH = 3
D_FF = 4 * D_MODEL
IN_SPLITS = (
    A_HEADS * A_HEAD_DIM, A_KV_HEADS * A_HEAD_DIM, A_KV_HEADS * A_HEAD_DIM,
    2 * B_HEADS * B_HEAD_DIM, 2 * B_HEADS * B_HEAD_DIM, B_HEADS * B_V_DIM,
    C_Q_LORA, C_KV_LORA, C_ROPE,
    N_BRANCH * D_MODEL,
)
IN_COLS = sum(IN_SPLITS)

LOG2E = math.log2(math.e)
SUB = 256
N_STREAMS = 4
N_STREAMS_ALIBI = 2
TK = 512
TAIL = TK
ROW_TILE = 512
ONES_ROWS = 16
MASK_SCORE = -30000.0
EXP_ZERO = -110.0
QK_BOUND_SLACK = 1.02
VMEM_LIMIT = 52 * 1024 * 1024


def _cparams(sem):
    return pltpu.CompilerParams(dimension_semantics=sem, vmem_limit_bytes=VMEM_LIMIT)


def _dot_tn(a, w):
    return lax.dot_general(a.astype(BF16), w, (((0,), (0,)), ((), ())),
                           preferred_element_type=F32)


def _merge_kernel(lam_ref, oa_ref, ob_ref, oc_ref, gx_ref, wg_ref, bg_ref, gsub_ref, x_ref,
                  wa_ref, wb_ref, wc_ref, wo_ref, out_ref, *, n_valid, lam_init):
    pa = _dot_tn(oa_ref[...], wa_ref[...])
    heads = []
    for h in range(B_HEADS):
        o1 = ob_ref[(2 * h) * B_V_DIM:(2 * h + 1) * B_V_DIM, :]
        o2 = ob_ref[(2 * h + 1) * B_V_DIM:(2 * h + 2) * B_V_DIM, :]
        d = o1 - lam_ref[0] * o2
        r = lax.rsqrt(jnp.mean(d * d, axis=0, keepdims=True) + NORM_EPS)
        heads.append(d * r * gsub_ref[...] * (1.0 - lam_init))
    pb = _dot_tn(jnp.concatenate(heads, axis=0), wb_ref[...])
    pc = _dot_tn(oc_ref[...], wc_ref[...])
    x = x_ref[...]
    hn = (x * lax.rsqrt(jnp.mean(x * x, axis=-1, keepdims=True) + NORM_EPS) * gx_ref[...]).astype(BF16)
    g = jax.nn.sigmoid(jnp.dot(hn, wg_ref[...], preferred_element_type=F32) + bg_ref[...])
    merged = (g[:, :D_MODEL] * pa + g[:, D_MODEL:2 * D_MODEL] * pb
              + g[:, 2 * D_MODEL:] * pc)
    y = x + jnp.dot(merged.astype(BF16), wo_ref[...], preferred_element_type=F32)
    rows = pl.program_id(1) * ROW_TILE + lax.broadcasted_iota(jnp.int32, (ROW_TILE, 1), 0)
    out_ref[...] = jnp.where(rows < n_valid, y, 0.0)


def merge_branches(lam, oa, ob, oc, g_x, w_gate, b_gate, g_sub, x, wa, wb, wc, wo, *, n_valid,
                   lam_init):
    B, N, _ = x.shape
    col = lambda a: pl.BlockSpec((None, a.shape[1], ROW_TILE), lambda b, i: (b, 0, i))
    row = lambda c: pl.BlockSpec((None, ROW_TILE, c), lambda b, i: (b, i, 0))
    full = lambda a: pl.BlockSpec(a.shape, lambda b, i: (0, 0))
    return pl.pallas_call(
        functools.partial(_merge_kernel, n_valid=n_valid, lam_init=lam_init),
        out_shape=jax.ShapeDtypeStruct((B, N, D_MODEL), F32),
        grid=(B, N // ROW_TILE),
        in_specs=[pl.BlockSpec(memory_space=pltpu.SMEM), col(oa), col(ob), col(oc),
                  full(g_x), full(w_gate),
                  pl.BlockSpec((1, N_BRANCH * D_MODEL), lambda b, i: (0, 0)),
                  pl.BlockSpec((B_V_DIM, 1), lambda b, i: (0, 0)),
                  row(D_MODEL), full(wa), full(wb), full(wc), full(wo)],
        out_specs=row(D_MODEL),
        compiler_params=_cparams(("parallel", "parallel")),
        name="merge_branches",
    )(lam.reshape(1), oa, ob, oc, g_x, w_gate, b_gate.reshape(1, -1), g_sub.reshape(-1, 1), x,
      wa, wb, wc, wo)


def _mlp_kernel(x_ref, g_ref, wu_ref, wd_ref, out_ref, *, ff_chunk):
    x = x_ref[...]
    r = lax.rsqrt(jnp.mean(x * x, axis=-1, keepdims=True) + NORM_EPS)
    h = (x * r * g_ref[...]).astype(BF16)
    acc = x
    for c in range(D_FF // ff_chunk):
        u = jnp.dot(h, wu_ref[:, c * ff_chunk:(c + 1) * ff_chunk],
                    preferred_element_type=F32)
        a = jnp.square(jnp.maximum(u, 0.0)).astype(BF16)
        acc = acc + jnp.dot(a, wd_ref[c * ff_chunk:(c + 1) * ff_chunk, :],
                            preferred_element_type=F32)
    out_ref[...] = acc


def mlp_block(x, g, wu, wd):
    R = x.shape[0]
    return pl.pallas_call(
        functools.partial(_mlp_kernel, ff_chunk=1024),
        out_shape=jax.ShapeDtypeStruct((R, D_MODEL), F32),
        grid=(R // ROW_TILE,),
        in_specs=[
            pl.BlockSpec((ROW_TILE, D_MODEL), lambda i: (i, 0)),
            pl.BlockSpec((1, D_MODEL), lambda i: (0, 0)),
            pl.BlockSpec(wu.shape, lambda i: (0, 0)),
            pl.BlockSpec(wd.shape, lambda i: (0, 0)),
        ],
        out_specs=pl.BlockSpec((ROW_TILE, D_MODEL), lambda i: (i, 0)),
        compiler_params=_cparams(("parallel",)),
        name="mlp_block",
    )(x, g.reshape(1, -1), wu, wd)


def _dot_nt(a, b):
    return lax.dot_general(a, b, (((1,), (1,)), ((), ())), preferred_element_type=F32)


def _attn_kernel(*refs, n_chunks, dv, feat0, alibi, meta_tile, n_streams):
    if alibi:
        sigma_ref, reach_ref, q_ref, kx_ref, vt_ref = refs[:5]
    else:
        q_ref, kx_ref, vt_ref = refs[:3]
    o_ref = refs[-8]
    m_ref, acc_ref, s_ring, bm_ring, p_ring, al_ring, q2_ref = refs[-7:]
    tqt = n_streams * SUB
    qi = pl.program_id(2)
    streams = range(n_streams)

    m_ref[...] = jnp.full(m_ref.shape, -1e30, F32)
    acc_ref[...] = jnp.zeros(acc_ref.shape, F32)

    ex = jnp.exp if alibi else jnp.exp2

    def q_form(st, form):
        if alibi:
            return q2_ref[form, st]
        return q_ref[st * SUB:(st + 1) * SUB, :]

    if alibi:
        for st in streams:
            q = q_ref[st * SUB:(st + 1) * SUB, :]
            lane = lax.broadcasted_iota(jnp.int32, q.shape, 1)
            q2_ref[0, st] = q
            q2_ref[1, st] = jnp.where((lane >= feat0) & (lane < feat0 + 4), -q, q)
        sigma = sigma_ref[pl.program_id(1)]
        if meta_tile:
            n_diag, j_before = 1, jnp.int32(0)
        else:
            n_diag, j_before = max(1, tqt // TK), (qi * tqt) // TK
    else:
        n_diag = 0
    n_pure = (n_chunks if meta_tile else n_chunks + 1 - n_diag) if alibi else n_chunks + 1
    if alibi:
        reach = reach_ref[pl.program_id(1)]
        if meta_tile:
            lo, hi = jnp.int32(0), jnp.minimum(n_pure, reach + 1)
        else:
            lo = jnp.maximum(0, j_before - reach)
            hi = jnp.minimum(n_pure, j_before + reach + 2)
        odd = (hi - lo) % 2
        grow_lo = (odd == 1) & (lo > 0)
        lo = lo - jnp.where(grow_lo, 1, 0)
        hi = hi + jnp.where((odd == 1) & jnp.logical_not(grow_lo), 1, 0)
        cnt = hi - lo
        extras = [n_chunks] if meta_tile else [j_before + d for d in range(n_diag)]
    else:
        lo, cnt = 0, n_pure - n_pure % 2
        extras = [n_chunks] if n_pure % 2 else []
    assert n_pure % 2 == 0 or not alibi

    def base_i(st):
        return jnp.int32(0) if meta_tile else N_META + qi * tqt + st * SUB

    def delta(st, b):
        return (base_i(st) - jnp.where(b == n_chunks, 0, N_META + b * TK)).astype(F32)

    def chunk_rows(b):
        if isinstance(b, int):
            return pl.ds(b * TK, TK)
        return pl.ds(pl.multiple_of(b * TK, TK), TK)

    def block(pos, extra=None):
        if extra is not None:
            return extras[extra], None, alibi
        v = lo + pos
        if not alibi:
            return v, None, False
        if meta_tile:
            return v, True, False
        after = v > j_before
        b = jnp.where(v == 0, n_chunks, v - 1 + jnp.where(after, n_diag, 0))
        return b, after, False

    def stage_a(blk, slot):
        b, after, diag = blk
        kx = kx_ref[chunk_rows(b), :]
        for st in streams:
            if diag:
                c = sigma * delta(st, b)
                s = jnp.minimum(_dot_nt(kx, q_form(st, 0)) - c, _dot_nt(kx, q_form(st, 1)) + c)
            elif alibi:
                s = _dot_nt(kx, q_form(st, 1 if after is True else after.astype(jnp.int32)))
            else:
                s = _dot_nt(kx, q_form(st, 0))
            s_ring[slot, st] = s
            bm_ring[slot, st] = jnp.max(s, axis=0, keepdims=True)

    def stage_b(blk, slot):
        b, after, diag = blk
        for st in streams:
            shift = 0.0
            if alibi and not diag:
                d = sigma * delta(st, b)
                shift = jnp.where(after, d, -d)
            m_old = m_ref[st]
            m_new = jnp.maximum(m_old, bm_ring[slot, st] + shift)
            al_ring[slot, st] = ex(m_old - m_new)
            p_ring[slot, st] = ex(s_ring[slot, st] - (m_new - shift)).astype(BF16)
            m_ref[st] = m_new

    def stage_c(blk, slot):
        b, _, _ = blk
        vt = vt_ref[:, chunk_rows(b)]
        for st in streams:
            acc_ref[st] = acc_ref[st] * al_ring[slot, st] + jnp.dot(
                vt, p_ring[slot, st], preferred_element_type=F32)

    stage_a(block(0), 0)
    stage_a(block(1), 1)
    stage_b(block(0), 0)

    def pair(tp, carry):
        w = 2 * tp
        stage_a(block(w + 2), 0)
        stage_b(block(w + 1), 1)
        stage_c(block(w), 0)
        stage_a(block(w + 3), 1)
        stage_b(block(w + 2), 0)
        stage_c(block(w + 1), 1)
        return carry

    lax.fori_loop(0, cnt // 2 - 1, pair, 0)

    def at(k):
        return block(cnt + k) if k < 0 else block(None, extra=k)

    for k in range(len(extras) + 2):
        if k < len(extras):
            stage_a(at(k), k % 2)
        if k - 1 < len(extras):
            stage_b(at(k - 1), (k - 1) % 2)
        stage_c(at(k - 2), k % 2)

    for st in streams:
        acc = acc_ref[st]
        o_ref[:, st * SUB:(st + 1) * SUB] = acc[:dv, :] / acc[dv:dv + 1, :]


def _flash_call(q, kx, vt, sigma, *, heads, dv, feat0, meta_tile, n_streams, q0, n_q, prev=None):
    B, N, _ = q.shape
    H, Kd = heads, q.shape[2] // heads
    G, Gv, dvx = kx.shape[2] // Kd, vt.shape[1], vt.shape[2]
    n_chunks = (N - TAIL) // TK
    assert n_chunks * TK + TAIL == N
    tqt = n_streams * SUB
    assert n_q % tqt == 0 and q0 % tqt == 0
    t0 = q0 // tqt
    alibi = sigma is not None
    in_specs = [
        pl.BlockSpec((None, tqt, Kd), lambda b, h, i: (b, t0 + i, h)),
        pl.BlockSpec((None, N, Kd), lambda b, h, i: (b, 0, h // (H // G))),
        pl.BlockSpec((None, None, dvx, N), lambda b, h, i: (b, h // (H // Gv), 0, 0)),
    ]
    args = [q, kx, vt]
    if alibi:
        in_specs = [pl.BlockSpec(memory_space=pltpu.SMEM)] * 2 + in_specs
        args = list(sigma) + args
    aliases = {}
    if prev is not None:
        aliases = {len(args): 0}
        in_specs = in_specs + [pl.BlockSpec(memory_space=pl.ANY)]
        args = args + [prev]
    return pl.pallas_call(
        functools.partial(_attn_kernel, n_chunks=n_chunks, dv=dv, feat0=feat0, alibi=alibi,
                          meta_tile=meta_tile, n_streams=n_streams),
        out_shape=jax.ShapeDtypeStruct((B, H, dv, N), F32),
        grid=(B, H, n_q // tqt),
        in_specs=in_specs,
        out_specs=pl.BlockSpec((None, None, dv, tqt), lambda b, h, i: (b, h, 0, t0 + i)),
        input_output_aliases=aliases,
        scratch_shapes=[pltpu.VMEM((n_streams, 1, SUB), F32),
                        pltpu.VMEM((n_streams, dvx, SUB), F32),
                        pltpu.VMEM((2, n_streams, TK, SUB), F32),
                        pltpu.VMEM((2, n_streams, 1, SUB), F32),
                        pltpu.VMEM((2, n_streams, TK, SUB), BF16),
                        pltpu.VMEM((2, n_streams, 1, SUB), F32),
                        pltpu.VMEM((2, n_streams, SUB, Kd), BF16)],
        compiler_params=_cparams(("parallel", "parallel", "arbitrary")),
        name=("flash_alibi" if alibi else "flash_plain") + ("_meta" if meta_tile else ""),
    )(*args)


def flash_attention(q, kx, vt, *, heads, dv, feat0, sigma=None):
    N = q.shape[1]
    n = N - TAIL
    kw = dict(heads=heads, dv=dv, feat0=feat0)
    n_streams = N_STREAMS if sigma is None else N_STREAMS_ALIBI
    o = _flash_call(q, kx, vt, sigma, meta_tile=False, n_streams=n_streams, q0=0, n_q=n, **kw)
    o = _flash_call(q, kx, vt, sigma, meta_tile=True, n_streams=1, q0=n, n_q=SUB, prev=o, **kw)
    return o.reshape(o.shape[0], -1, N)


_ZQA, _ZQAR, _ZKA, _ZKAR, _ZVA = 0, 512, 1024, 1152, 1280
_ZQB, _ZKB, _ZVB = 1408, 1920, 2432
_ZCQ, _ZCKV, _ZKPE = 2944, 3200, 3328
DQK_C = C_NOPE + C_ROPE


def _rot_perm(d, block):
    j = np.arange(d)
    half = block // 2
    lower = (j % block) < half
    return np.where(lower, j + half, j - half), np.where(lower, -1.0, 1.0)


def _block_ones(n, width):
    g = np.arange(n) // width
    return jnp.asarray(g[:, None] == g[None, :], BF16)


def _placement(n_in, n_out, src_width, dst_stride, dst_off=0):
    p = np.zeros((n_in, n_out), np.float32)
    j = np.arange(n_in)
    p[j, (j // src_width) * dst_stride + dst_off + j % src_width] = 1.0
    return jnp.asarray(p, BF16)


def _group_sum(x2, ones):
    hi = x2.astype(BF16)
    lo = (x2 - hi.astype(F32)).astype(BF16)
    return (jnp.dot(hi, ones, preferred_element_type=F32)
            + jnp.dot(lo, ones, preferred_element_type=F32))


def _place(x_bf16, p):
    return jnp.dot(x_bf16, p, preferred_element_type=F32)


def _tile_lanes(x, reps):
    return jnp.concatenate([x] * reps, axis=1)


def _prep_kernel(x_ref, gx_ref, w_ref, tab_ref, qfb_ref, kfb_ref, gains_ref, gcq_ref, gckv_ref, wqb_ref, wkvb_ref,
                 s512_ref, s128_ref, s128x512_ref, e_pn_ref, e_np_ref, s64x256_ref,
                 pq_ref, pk_ref, pva_ref, ppe_ref,
                 qa_ref, kxa_ref, vta_ref, qb_ref, kxb_ref, vtb_ref, qc_ref, kxc_ref, vtc_ref,
                 *, n_valid):
    tm = x_ref.shape[0]
    x = x_ref[...]
    hn = (x * lax.rsqrt(jnp.mean(x * x, axis=-1, keepdims=True) + NORM_EPS) * gx_ref[...]).astype(BF16)

    def zcols(off, width):
        return jnp.dot(hn, w_ref[:, off:off + width], preferred_element_type=F32)

    rows = pl.program_id(1) * tm + lax.broadcasted_iota(jnp.int32, (tm, 1), 0)
    valid = (rows < n_valid).astype(F32)
    lane128 = lax.broadcasted_iota(jnp.int32, (1, 128), 1)
    flag_col = (lane128 == 64).astype(F32)
    one_col0 = (lane128 == 0).astype(F32)
    g = lambda i, w: gains_ref[i:i + 1, :w]
    cos_a, sin_a = tab_ref[:, 0:128], tab_ref[:, 128:256]
    cos_c, sin_c = tab_ref[:, 256:384], tab_ref[:, 384:512]
    cs_k = tab_ref[:, 512:640]
    rs = lambda ms: lax.rsqrt(ms + NORM_EPS)

    def v_transposed(v_ext, rows_out):
        return (v_ext * valid).T[:rows_out].astype(BF16)

    q, qr = zcols(_ZQA, 512), zcols(_ZQAR, 512)
    r = rs(_group_sum(q * q, s512_ref[...]) * (1.0 / A_HEAD_DIM))
    qo = ((q * g(0, 512) * _tile_lanes(cos_a, 4) + qr * g(1, 512) * _tile_lanes(sin_a, 4))
          * (r * (A_HEAD_DIM ** -0.5 * LOG2E)))
    mask_row = MASK_SCORE * _tile_lanes(flag_col, 8)
    qa_ref[...] = (_place(qo.astype(BF16), pq_ref[...]) + mask_row).astype(BF16)
    k, kr = zcols(_ZKA, 128), zcols(_ZKAR, 128)
    r = rs(_group_sum(k * k, s128_ref[...]) * (1.0 / A_HEAD_DIM))
    ko = (k * g(2, 128) * cos_a + kr * g(3, 128) * sin_a) * (r * valid)
    kxa_ref[...] = (_place(ko.astype(BF16), pk_ref[...])
                    + (1.0 - valid) * _tile_lanes(flag_col, 2)).astype(BF16)
    v = zcols(_ZVA, 128).astype(BF16)
    for h in range(A_KV_HEADS):
        v_ext = _place(v, pva_ref[h]) + flag_col
        vta_ref[h] = v_transposed(v_ext, A_HEAD_DIM + ONES_ROWS)

    q = zcols(_ZQB, 512)
    r = rs(_group_sum(q * q, s512_ref[...]) * (1.0 / B_HEAD_DIM))
    qo = q * g(4, 512) * (r * (B_HEAD_DIM ** -0.5))
    qb_ref[...] = (_place(qo.astype(BF16), pq_ref[...]) + qfb_ref[...]).astype(BF16)
    k = zcols(_ZKB, 512)
    r = rs(_group_sum(k * k, s512_ref[...]) * (1.0 / B_HEAD_DIM))
    ko = k * g(5, 512) * (r * valid)
    kxb_ref[...] = (_place(ko.astype(BF16), pq_ref[...]) + kfb_ref[...] * valid
                    + (1.0 - valid) * _tile_lanes((lane128 == 68).astype(F32), 8)).astype(BF16)
    for h in range(B_HEADS):
        v_h = zcols(_ZVB + h * B_V_DIM, B_V_DIM)
        v_ext = jnp.concatenate([v_h, jnp.broadcast_to(one_col0, (tm, 128))], axis=1)
        vtb_ref[h] = v_transposed(v_ext, B_V_DIM + ONES_ROWS)

    cq = zcols(_ZCQ, C_Q_LORA)
    cqn = cq * rs(jnp.mean(cq * cq, axis=-1, keepdims=True)) * gcq_ref[...]
    q2 = jnp.dot(cqn.astype(BF16), wqb_ref[...], preferred_element_type=F32)
    nope, pe, per = q2[:, :512], q2[:, 512:768], q2[:, 768:1024]
    nn, pp = nope * nope, pe * pe
    r_n = rs((_group_sum(nn, s128x512_ref[...]) + _group_sum(pp, e_pn_ref[...])) * (1.0 / DQK_C))
    r_p = rs((_group_sum(pp, s64x256_ref[...]) + _group_sum(nn, e_np_ref[...])) * (1.0 / DQK_C))
    sc = DQK_C ** -0.5 * LOG2E
    q_nope = (nope * g(6, 512) * (r_n * sc)).astype(BF16)
    q_pe = ((pe * g(7, 256) * _tile_lanes(cos_c, 2) + per * g(8, 256) * _tile_lanes(sin_c, 2))
            * (r_p * sc))
    pe_slots = (_place(q_pe.astype(BF16), ppe_ref[...])
                + MASK_SCORE * _tile_lanes(flag_col, 4)).astype(BF16)
    qc_ref[...] = jnp.concatenate(
        [t for h in range(C_HEADS)
         for t in (q_nope[:, 128 * h:128 * (h + 1)], pe_slots[:, 128 * h:128 * (h + 1)])], axis=1)

    ckv = zcols(_ZCKV, C_KV_LORA)
    ckvn = ckv * rs(jnp.mean(ckv * ckv, axis=-1, keepdims=True)) * gckv_ref[...]
    kv = jnp.dot(ckvn.astype(BF16), wkvb_ref[...], preferred_element_type=F32)
    k_nope = kv[:, :512]
    kk = zcols(_ZKPE, 128)
    pe_ss = jnp.sum(kk[:, :64] * kk[:, :64], axis=-1, keepdims=True)
    r_k = rs((_group_sum(k_nope * k_nope, s128x512_ref[...]) + pe_ss) * (1.0 / DQK_C))
    k_no = (k_nope * g(9, 512) * (r_k * valid)).astype(BF16)
    t = kk * g(10, 128) * cs_k
    k_pe = jnp.where(lane128 < 64, t + pltpu.roll(t, 64, axis=1), 0.0)
    pad_flag = ((1.0 - valid) * flag_col).astype(BF16)
    kxc_ref[...] = jnp.concatenate(
        [t2 for h in range(C_HEADS)
         for t2 in (k_no[:, 128 * h:128 * (h + 1)],
                    (k_pe * r_k[:, 128 * h:128 * h + 1] * valid).astype(BF16) + pad_flag)], axis=1)
    for h in range(C_HEADS):
        v_h = kv[:, 512 + h * C_V:512 + (h + 1) * C_V]
        v_ext = jnp.concatenate([v_h, jnp.broadcast_to(one_col0, (tm, 128))], axis=1)
        vtc_ref[h] = v_transposed(v_ext, C_V + ONES_ROWS)


def _positions(n):
    pad = TAIL - N_META
    rows = n // GRID_W
    row = np.concatenate([np.repeat(np.arange(rows), GRID_W), np.full(N_META, -1.0), np.zeros(pad)])
    col = np.concatenate([np.tile(np.arange(GRID_W), rows), np.arange(N_META), np.zeros(pad)])
    lin = np.concatenate([N_META + np.arange(n), np.arange(N_META), np.zeros(pad)])
    f = lambda a: jnp.asarray(a, F32)
    return f(row), f(col), f(lin)


def _rope_tables(n):
    row, col, lin = _positions(n)

    def angles(pos, d):
        inv = ROPE_THETA ** (-2.0 * jnp.arange(d // 2, dtype=F32) / d)
        ang = pos[:, None] * inv[None, :]
        return jnp.concatenate([ang, ang], axis=1)

    ang_a = jnp.concatenate([angles(row, A_HEAD_DIM // 2), angles(col, A_HEAD_DIM // 2)], axis=1)
    ang_c = angles(lin, C_ROPE)
    two = lambda t: jnp.concatenate([t, t], axis=1)
    return jnp.concatenate([two(jnp.cos(ang_a)), two(jnp.sin(ang_a)), two(jnp.cos(ang_c)),
                            two(jnp.sin(ang_c)), jnp.cos(ang_c), jnp.sin(ang_c)], axis=1)


def _alibi_reach(sigma, qk_bound):
    dist = (2.0 * QK_BOUND_SLACK * qk_bound - EXP_ZERO) / sigma
    return jnp.clip(jnp.floor(dist / TK), 0, 1 << 20).astype(jnp.int32)


def _alibi_tables():
    slopes = 2.0 ** (-8.0 * np.arange(1, B_HEADS + 1) / B_HEADS)
    sig = np.repeat(slopes, 2)
    idx = np.arange(ROW_TILE)
    ii, jj = idx % SUB, idx % TK
    qf = np.zeros((ROW_TILE, 2 * B_HEADS, 128), np.float32)
    kf = np.zeros((ROW_TILE, 2 * B_HEADS, 128), np.float32)
    ii_hi = (ii // 256) * 256
    qf[:, :, 64] = -ii_hi[:, None] * sig[None, :]
    qf[:, :, 65] = -(ii - ii_hi)[:, None] * sig[None, :]
    qf[:, :, 66] = 256.0 * sig[None, :]
    qf[:, :, 67] = sig[None, :]
    qf[:, :, 68] = MASK_SCORE
    kf[:, :, 64] = 1.0
    kf[:, :, 65] = 1.0
    kf[:, :, 66] = (jj // 256)[:, None]
    kf[:, :, 67] = (jj % 256)[:, None]
    flat = lambda a: jnp.asarray(a.reshape(ROW_TILE, -1), F32)
    return jnp.asarray(sig, F32), flat(qf), flat(kf)


def _prep_params(l, p):
    w = p['w_in'][l]
    cuts = np.concatenate([[0], np.cumsum(IN_SPLITS)])
    sec = lambda i: w[:, int(cuts[i]):int(cuts[i + 1])]
    qa, ka, va, qb, kb, vb, cq, ckv, kpe, gates = [sec(i) for i in range(10)]
    perm_a, sign_a = _rot_perm(A_HEAD_DIM, A_HEAD_DIM // 2)
    perm_c, sign_c = _rot_perm(C_ROPE, C_ROPE)

    def rot_cols(wm, perm, sign, heads):
        d = len(perm)
        cols = (np.arange(heads)[:, None] * d + perm[None, :]).reshape(-1)
        return wm[:, cols] * jnp.asarray(np.tile(sign, heads), F32)

    w_att = jnp.concatenate(
        [qa, rot_cols(qa, perm_a, sign_a, A_HEADS), ka, rot_cols(ka, perm_a, sign_a, A_KV_HEADS), va,
         qb, kb, vb, cq, ckv, kpe, rot_cols(kpe, perm_c, sign_c, 1)], axis=1).astype(BF16)
    w_gate = gates.astype(BF16)

    wq = p['c_w_q_b'][l].reshape(C_Q_LORA, C_HEADS, DQK_C)
    q_nope = wq[:, :, :C_NOPE].reshape(C_Q_LORA, -1)
    q_pe = wq[:, :, C_NOPE:].reshape(C_Q_LORA, -1)
    w_qb = jnp.concatenate([q_nope, q_pe, rot_cols(q_pe, perm_c, sign_c, C_HEADS)], 1).astype(BF16)
    wkv = p['c_w_kv_b'][l].reshape(C_KV_LORA, C_HEADS, C_NOPE + C_V)
    w_kvb = jnp.concatenate([wkv[:, :, :C_NOPE].reshape(C_KV_LORA, -1),
                             wkv[:, :, C_NOPE:].reshape(C_KV_LORA, -1)], 1).astype(BF16)

    tile = lambda v, r: jnp.tile(v, r)
    gaq, gak = p['a_q_norm_g'][l], p['a_k_norm_g'][l]
    gcq, gck = p['c_q_norm_g'][l], p['c_k_norm_g'][l]
    rows = [tile(gaq, 8), tile(gaq[perm_a], 8), tile(gak, 2), tile(gak[perm_a], 2),
            tile(p['b_q_norm_g'][l], 8), tile(p['b_k_norm_g'][l], 8),
            tile(gcq[:C_NOPE], 4), tile(gcq[C_NOPE:], 4), tile(gcq[C_NOPE:][perm_c], 4),
            tile(gck[:C_NOPE], 4), jnp.concatenate([gck[C_NOPE:], gck[C_NOPE:][perm_c]])]
    gains = jnp.stack([jnp.pad(r, (0, 512 - r.shape[0])) for r in rows]
                      + [jnp.zeros((512,), F32)] * 5)
    pva = jnp.stack([_placement(128, 128, 64, 0) * (np.arange(128)[:, None] // 64 == h)
                     for h in range(A_KV_HEADS)]).astype(BF16)
    consts = (_block_ones(512, 64), _block_ones(128, 64), _block_ones(512, 128),
              jnp.asarray(np.arange(256)[:, None] // 64 == np.arange(512)[None, :] // 128, BF16),
              jnp.asarray(np.arange(512)[:, None] // 128 == np.arange(256)[None, :] // 64, BF16),
              _block_ones(256, 64),
              _placement(512, 1024, 64, 128), _placement(128, 256, 64, 128), pva,
              _placement(256, 512, 64, 128))
    return w_gate, w_att, w_qb, w_kvb, gains, consts


def prep_attention_inputs(x, g_x, w_att, tabs, qfb, kfb, gains, g_cq, g_ckv, w_qb, w_kvb, consts, *,
                          n_valid):
    B, N, _ = x.shape
    tm = ROW_TILE
    row = lambda c: pl.BlockSpec((None, tm, c), lambda b, i: (b, i, 0))
    const = lambda a: pl.BlockSpec(a.shape, lambda b, i: (0,) * a.ndim)
    vt_spec = lambda g, d: pl.BlockSpec((None, g, d, tm), lambda b, i: (b, 0, 0, i))
    vt_shape = lambda g, d: jax.ShapeDtypeStruct((B, g, d, N), BF16)
    tok = lambda c: jax.ShapeDtypeStruct((B, N, c), BF16)
    small = [gains, g_cq.reshape(1, -1), g_ckv.reshape(1, -1), w_qb, w_kvb, *consts]
    return pl.pallas_call(
        functools.partial(_prep_kernel, n_valid=n_valid),
        out_shape=[tok(1024), tok(256), vt_shape(A_KV_HEADS, A_HEAD_DIM + ONES_ROWS),
                   tok(1024), tok(1024), vt_shape(B_HEADS, B_V_DIM + ONES_ROWS),
                   tok(1024), tok(1024), vt_shape(C_HEADS, C_V + ONES_ROWS)],
        grid=(B, N // tm),
        in_specs=[row(D_MODEL), const(g_x), const(w_att),
                  pl.BlockSpec((tm, tabs.shape[1]), lambda b, i: (i, 0)),
                  const(qfb), const(kfb)] + [const(a) for a in small],
        out_specs=[row(1024), row(256), vt_spec(A_KV_HEADS, A_HEAD_DIM + ONES_ROWS),
                   row(1024), row(1024), vt_spec(B_HEADS, B_V_DIM + ONES_ROWS),
                   row(1024), row(1024), vt_spec(C_HEADS, C_V + ONES_ROWS)],
        compiler_params=_cparams(("parallel", "parallel")),
        name="prep_attention_inputs",
    )(x, g_x, w_att, tabs, qfb, kfb, *small)


def _layer(x, tabs, l, p):
    B, N, _ = x.shape
    n_valid = N - TAIL + N_META
    w_gate, w_att, w_qb, w_kvb, gains, consts = _prep_params(l, p)
    g_x = p['attn_norm_g'][l].reshape(1, D_MODEL)
    sig, qfb, kfb = _alibi_tables()
    qa, kxa, vta, qb, kxb, vtb, qc, kxc, vtc = prep_attention_inputs(
        x, g_x, w_att, tabs, qfb, kfb, gains, p['c_q_a_norm_g'][l], p['c_kv_a_norm_g'][l], w_qb, w_kvb,
        consts, n_valid=n_valid)
    reach = _alibi_reach(sig, B_HEAD_DIM ** 0.5 * jnp.max(jnp.abs(p['b_q_norm_g'][l]))
                         * jnp.max(jnp.abs(p['b_k_norm_g'][l])))
    oa = flash_attention(qa, kxa, vta, heads=A_HEADS, dv=A_HEAD_DIM, feat0=A_HEAD_DIM)
    ob = flash_attention(qb, kxb, vtb, heads=2 * B_HEADS, dv=B_V_DIM, feat0=B_HEAD_DIM,
                         sigma=(sig, reach))
    oc = flash_attention(qc, kxc, vtc, heads=C_HEADS, dv=C_V, feat0=DQK_C)

    lam_init = 0.8 - 0.6 * math.exp(-0.3 * l)
    lam = (jnp.exp(jnp.sum(p['b_lambda_q1'][l] * p['b_lambda_k1'][l]))
           - jnp.exp(jnp.sum(p['b_lambda_q2'][l] * p['b_lambda_k2'][l])) + lam_init)
    x = merge_branches(lam, oa, ob, oc, g_x, w_gate, p['b_gate'][l], p['b_subln_g'][l], x,
                       p['w_branch_a'][l].astype(BF16), p['w_branch_b'][l].astype(BF16),
                       p['w_branch_c'][l].astype(BF16), p['w_out'][l].astype(BF16),
                       n_valid=n_valid, lam_init=lam_init)
    x2 = mlp_block(x.reshape(B * N, D_MODEL), p['mlp_norm_g'][l], p['w_up'][l].astype(BF16),
                   p['w_down'][l].astype(BF16))
    return x2.reshape(B, N, D_MODEL)


def _trunk(x, meta_tokens, p):
    B, n, _ = x.shape
    assert n % TK == 0 and n % (N_STREAMS * SUB) == 0 and n % GRID_W == 0
    tabs = _rope_tables(n)
    meta = jnp.broadcast_to(meta_tokens[None], (B, N_META, D_MODEL))
    h = jnp.concatenate([x, meta, jnp.zeros((B, TAIL - N_META, D_MODEL), F32)], axis=1)
    for l in range(DEPTH):
        h = _layer(h, tabs, l, p)
    return h[:, :n]


def kernel(x_prompt, x_sample, meta_tokens, attn_norm_g, w_in, b_gate, a_q_norm_g, a_k_norm_g, b_q_norm_g, b_k_norm_g, b_lambda_q1, b_lambda_k1, b_lambda_q2, b_lambda_k2, b_subln_g, c_q_a_norm_g, c_w_q_b, c_kv_a_norm_g, c_w_kv_b, c_q_norm_g, c_k_norm_g, w_branch_a, w_branch_b, w_branch_c, w_out, mlp_norm_g, w_up, w_down):
    p = dict(attn_norm_g=attn_norm_g, w_in=w_in, b_gate=b_gate,
             a_q_norm_g=a_q_norm_g, a_k_norm_g=a_k_norm_g,
             b_q_norm_g=b_q_norm_g, b_k_norm_g=b_k_norm_g,
             b_lambda_q1=b_lambda_q1, b_lambda_k1=b_lambda_k1,
             b_lambda_q2=b_lambda_q2, b_lambda_k2=b_lambda_k2, b_subln_g=b_subln_g,
             c_q_a_norm_g=c_q_a_norm_g, c_w_q_b=c_w_q_b,
             c_kv_a_norm_g=c_kv_a_norm_g, c_w_kv_b=c_w_kv_b,
             c_q_norm_g=c_q_norm_g, c_k_norm_g=c_k_norm_g,
             w_branch_a=w_branch_a, w_branch_b=w_branch_b, w_branch_c=w_branch_c,
             w_out=w_out, mlp_norm_g=mlp_norm_g, w_up=w_up, w_down=w_down)
    return (_trunk(x_prompt, meta_tokens, p), _trunk(x_sample, meta_tokens, p))
```

```python
import functools
import math

import numpy as np
import jax
import jax.numpy as jnp
from jax import lax
from jax.experimental import pallas as pl
from jax.experimental.pallas import tpu as pltpu

F32 = jnp.float32
BF16 = jnp.bfloat16

D_MODEL = 1024
DEPTH = 2
N_META = 16
GRID_W = 64
ROPE_THETA = 10000.0
NORM_EPS = 1e-6
A_HEADS, A_KV_HEADS, A_HEAD_DIM = 8, 2, 64
B_HEADS, B_HEAD_DIM = 4, 64
B_V_DIM = 2 * B_HEAD_DIM
C_HEADS, C_NOPE, C_ROPE, C_V = 4, 128, 64, 128
C_Q_LORA, C_KV_LORA = 256, 128
N_BRANCH = 3
D_FF = 4 * D_MODEL
IN_SPLITS = (
    A_HEADS * A_HEAD_DIM, A_KV_HEADS * A_HEAD_DIM, A_KV_HEADS * A_HEAD_DIM,
    2 * B_HEADS * B_HEAD_DIM, 2 * B_HEADS * B_HEAD_DIM, B_HEADS * B_V_DIM,
    C_Q_LORA, C_KV_LORA, C_ROPE,
    N_BRANCH * D_MODEL,
)
IN_COLS = sum(IN_SPLITS)

LOG2E = math.log2(math.e)
SUB = 256
N_STREAMS = 4
N_STREAMS_ALIBI = 2
TK = 512
TAIL = TK
ROW_TILE = 512
ONES_ROWS = 16
MASK_SCORE = -30000.0
EXP_ZERO = -110.0
QK_BOUND_SLACK = 1.02
VMEM_LIMIT = 52 * 1024 * 1024


def _cparams(sem):
    return pltpu.CompilerParams(dimension_semantics=sem, vmem_limit_bytes=VMEM_LIMIT)


def _dot_tn(a, w):
    return lax.dot_general(a.astype(BF16), w, (((0,), (0,)), ((), ())),
                           preferred_element_type=F32)


def _merge_kernel(lam_ref, oa_ref, ob_ref, oc_ref, gx_ref, wg_ref, bg_ref, gsub_ref, x_ref,
                  wa_ref, wb_ref, wc_ref, wo_ref, out_ref, *, n_valid, lam_init):
    pa = _dot_tn(oa_ref[...], wa_ref[...])
    heads = []
    for h in range(B_HEADS):
        o1 = ob_ref[(2 * h) * B_V_DIM:(2 * h + 1) * B_V_DIM, :]
        o2 = ob_ref[(2 * h + 1) * B_V_DIM:(2 * h + 2) * B_V_DIM, :]
        d = o1 - lam_ref[0] * o2
        r = lax.rsqrt(jnp.mean(d * d, axis=0, keepdims=True) + NORM_EPS)
        heads.append(d * r * gsub_ref[...] * (1.0 - lam_init))
    pb = _dot_tn(jnp.concatenate(heads, axis=0), wb_ref[...])
    pc = _dot_tn(oc_ref[...], wc_ref[...])
    x = x_ref[...]
    hn = (x * lax.rsqrt(jnp.mean(x * x, axis=-1, keepdims=True) + NORM_EPS) * gx_ref[...]).astype(BF16)
    g = jax.nn.sigmoid(jnp.dot(hn, wg_ref[...], preferred_element_type=F32) + bg_ref[...])
    merged = (g[:, :D_MODEL] * pa + g[:, D_MODEL:2 * D_MODEL] * pb
              + g[:, 2 * D_MODEL:] * pc)
    y = x + jnp.dot(merged.astype(BF16), wo_ref[...], preferred_element_type=F32)
    rows = pl.program_id(1) * ROW_TILE + lax.broadcasted_iota(jnp.int32, (ROW_TILE, 1), 0)
    out_ref[...] = jnp.where(rows < n_valid, y, 0.0)


def merge_branches(lam, oa, ob, oc, g_x, w_gate, b_gate, g_sub, x, wa, wb, wc, wo, *, n_valid,
                   lam_init):
    B, N, _ = x.shape
    col = lambda a: pl.BlockSpec((None, a.shape[1], ROW_TILE), lambda b, i: (b, 0, i))
    row = lambda c: pl.BlockSpec((None, ROW_TILE, c), lambda b, i: (b, i, 0))
    full = lambda a: pl.BlockSpec(a.shape, lambda b, i: (0, 0))
    return pl.pallas_call(
        functools.partial(_merge_kernel, n_valid=n_valid, lam_init=lam_init),
        out_shape=jax.ShapeDtypeStruct((B, N, D_MODEL), F32),
        grid=(B, N // ROW_TILE),
        in_specs=[pl.BlockSpec(memory_space=pltpu.SMEM), col(oa), col(ob), col(oc),
                  full(g_x), full(w_gate),
                  pl.BlockSpec((1, N_BRANCH * D_MODEL), lambda b, i: (0, 0)),
                  pl.BlockSpec((B_V_DIM, 1), lambda b, i: (0, 0)),
                  row(D_MODEL), full(wa), full(wb), full(wc), full(wo)],
        out_specs=row(D_MODEL),
        compiler_params=_cparams(("parallel", "parallel")),
        name="merge_branches",
    )(lam.reshape(1), oa, ob, oc, g_x, w_gate, b_gate.reshape(1, -1), g_sub.reshape(-1, 1), x,
      wa, wb, wc, wo)


def _mlp_kernel(x_ref, g_ref, wu_ref, wd_ref, out_ref, *, ff_chunk):
    x = x_ref[...]
    r = lax.rsqrt(jnp.mean(x * x, axis=-1, keepdims=True) + NORM_EPS)
    h = (x * r * g_ref[...]).astype(BF16)
    acc = x
    for c in range(D_FF // ff_chunk):
        u = jnp.dot(h, wu_ref[:, c * ff_chunk:(c + 1) * ff_chunk],
                    preferred_element_type=F32)
        a = jnp.square(jnp.maximum(u, 0.0)).astype(BF16)
        acc = acc + jnp.dot(a, wd_ref[c * ff_chunk:(c + 1) * ff_chunk, :],
                            preferred_element_type=F32)
    out_ref[...] = acc


def mlp_block(x, g, wu, wd):
    R = x.shape[0]
    return pl.pallas_call(
        functools.partial(_mlp_kernel, ff_chunk=1024),
        out_shape=jax.ShapeDtypeStruct((R, D_MODEL), F32),
        grid=(R // ROW_TILE,),
        in_specs=[
            pl.BlockSpec((ROW_TILE, D_MODEL), lambda i: (i, 0)),
            pl.BlockSpec((1, D_MODEL), lambda i: (0, 0)),
            pl.BlockSpec(wu.shape, lambda i: (0, 0)),
            pl.BlockSpec(wd.shape, lambda i: (0, 0)),
        ],
        out_specs=pl.BlockSpec((ROW_TILE, D_MODEL), lambda i: (i, 0)),
        compiler_params=_cparams(("parallel",)),
        name="mlp_block",
    )(x, g.reshape(1, -1), wu, wd)


def _dot_nt(a, b):
    return lax.dot_general(a, b, (((1,), (1,)), ((), ())), preferred_element_type=F32)


def _attn_kernel(*refs, n_chunks, dv, feat0, alibi, meta_tile, n_streams):
    if alibi:
        sigma_ref, reach_ref, q_ref, kx_ref, vt_ref = refs[:5]
    else:
        q_ref, kx_ref, vt_ref = refs[:3]
    n_scratch = 7 if alibi else 6
    o_ref = refs[-n_scratch - 1]
    m_ref, acc_ref, s_ring, bm_ring, p_ring, al_ring = refs[-n_scratch:][:6]
    q2_ref = refs[-1] if alibi else None
    tqt = n_streams * SUB
    qi = pl.program_id(2)
    streams = range(n_streams)

    m_ref[...] = jnp.full(m_ref.shape, -1e30, F32)
    acc_ref[...] = jnp.zeros(acc_ref.shape, F32)

    ex = jnp.exp if alibi else jnp.exp2

    def q_form(st, form):
        if alibi:
            return q2_ref[form, st]
        return q_ref[st * SUB:(st + 1) * SUB, :]

    if alibi:
        for st in streams:
            q = q_ref[st * SUB:(st + 1) * SUB, :]
            lane = lax.broadcasted_iota(jnp.int32, q.shape, 1)
            q2_ref[0, st] = q
            q2_ref[1, st] = jnp.where((lane >= feat0) & (lane < feat0 + 4), -q, q)
        sigma = sigma_ref[pl.program_id(1)]
        if meta_tile:
            n_diag, j_before = 1, jnp.int32(0)
        else:
            n_diag, j_before = max(1, tqt // TK), (qi * tqt) // TK
    else:
        n_diag = 0
    n_pure = (n_chunks if meta_tile else n_chunks + 1 - n_diag) if alibi else n_chunks + 1
    if alibi:
        reach = reach_ref[pl.program_id(1)]
        if meta_tile:
            lo, hi = jnp.int32(0), jnp.minimum(n_pure, reach + 1)
        else:
            lo = jnp.maximum(0, j_before - reach)
            hi = jnp.minimum(n_pure, j_before + reach + 2)
        odd = (hi - lo) % 2
        grow_lo = (odd == 1) & (lo > 0)
        lo = lo - jnp.where(grow_lo, 1, 0)
        hi = hi + jnp.where((odd == 1) & jnp.logical_not(grow_lo), 1, 0)
        cnt = hi - lo
        extras = [n_chunks] if meta_tile else [j_before + d for d in range(n_diag)]
    else:
        lo, cnt = 0, n_pure - n_pure % 2
        extras = [n_chunks] if n_pure % 2 else []
    assert n_pure % 2 == 0 or not alibi

    def base_i(st):
        return jnp.int32(0) if meta_tile else N_META + qi * tqt + st * SUB

    def delta(st, b):
        return (base_i(st) - jnp.where(b == n_chunks, 0, N_META + b * TK)).astype(F32)

    def chunk_rows(b):
        if isinstance(b, int):
            return pl.ds(b * TK, TK)
        return pl.ds(pl.multiple_of(b * TK, TK), TK)

    def block(pos, extra=None):
        if extra is not None:
            return extras[extra], None, alibi
        v = lo + pos
        if not alibi:
            return v, None, False
        if meta_tile:
            return v, True, False
        after = v > j_before
        b = jnp.where(v == 0, n_chunks, v - 1 + jnp.where(after, n_diag, 0))
        return b, after, False

    def stage_a(blk, slot):
        b, after, diag = blk
        kx = kx_ref[chunk_rows(b), :]
        for st in streams:
            if diag:
                c = sigma * delta(st, b)
                s = jnp.minimum(_dot_nt(kx, q_form(st, 0)) - c, _dot_nt(kx, q_form(st, 1)) + c)
            elif alibi:
                s = _dot_nt(kx, q_form(st, 1 if after is True else after.astype(jnp.int32)))
            else:
                s = _dot_nt(kx, q_form(st, 0))
            s_ring[slot, st] = s
            bm_ring[slot, st] = jnp.max(s, axis=0, keepdims=True)

    def stage_b(blk, slot):
        b, after, diag = blk
        for st in streams:
            shift = 0.0
            if alibi and not diag:
                d = sigma * delta(st, b)
                shift = jnp.where(after, d, -d)
            m_old = m_ref[st]
            m_new = jnp.maximum(m_old, bm_ring[slot, st] + shift)
            al_ring[slot, st] = ex(m_old - m_new)
            p_ring[slot, st] = ex(s_ring[slot, st] - (m_new - shift)).astype(BF16)
            m_ref[st] = m_new

    def stage_c(blk, slot):
        b, _, _ = blk
        vt = vt_ref[:, chunk_rows(b)]
        for st in streams:
            acc_ref[st] = acc_ref[st] * al_ring[slot, st] + jnp.dot(
                vt, p_ring[slot, st], preferred_element_type=F32)

    stage_a(block(0), 0)
    stage_a(block(1), 1)
    stage_b(block(0), 0)

    def pair(tp, carry):
        w = 2 * tp
        stage_a(block(w + 2), 0)
        stage_b(block(w + 1), 1)
        stage_c(block(w), 0)
        stage_a(block(w + 3), 1)
        stage_b(block(w + 2), 0)
        stage_c(block(w + 1), 1)
        return carry

    lax.fori_loop(0, cnt // 2 - 1, pair, 0)

    def at(k):
        return block(cnt + k) if k < 0 else block(None, extra=k)

    for k in range(len(extras) + 2):
        if k < len(extras):
            stage_a(at(k), k % 2)
        if k - 1 < len(extras):
            stage_b(at(k - 1), (k - 1) % 2)
        stage_c(at(k - 2), k % 2)

    for st in streams:
        acc = acc_ref[st]
        o_ref[:, st * SUB:(st + 1) * SUB] = acc[:dv, :] / acc[dv:dv + 1, :]


def _flash_call(q, kx, vt, sigma, *, heads, dv, feat0, meta_tile, n_streams, q0, n_q, prev=None):
    B, N, _ = q.shape
    H, Kd = heads, q.shape[2] // heads
    G, Gv, dvx = kx.shape[2] // Kd, vt.shape[1], vt.shape[2]
    n_chunks = (N - TAIL) // TK
    assert n_chunks * TK + TAIL == N
    tqt = n_streams * SUB
    assert n_q % tqt == 0 and q0 % tqt == 0
    t0 = q0 // tqt
    alibi = sigma is not None
    in_specs = [
        pl.BlockSpec((None, tqt, Kd), lambda b, h, i: (b, t0 + i, h)),
        pl.BlockSpec((None, N, Kd), lambda b, h, i: (b, 0, h // (H // G))),
        pl.BlockSpec((None, None, dvx, N), lambda b, h, i: (b, h // (H // Gv), 0, 0)),
    ]
    args = [q, kx, vt]
    if alibi:
        in_specs = [pl.BlockSpec(memory_space=pltpu.SMEM)] * 2 + in_specs
        args = list(sigma) + args
    aliases = {}
    if prev is not None:
        aliases = {len(args): 0}
        in_specs = in_specs + [pl.BlockSpec(memory_space=pl.ANY)]
        args = args + [prev]
    return pl.pallas_call(
        functools.partial(_attn_kernel, n_chunks=n_chunks, dv=dv, feat0=feat0, alibi=alibi,
                          meta_tile=meta_tile, n_streams=n_streams),
        out_shape=jax.ShapeDtypeStruct((B, H, dv, N), F32),
        grid=(B, H, n_q // tqt),
        in_specs=in_specs,
        out_specs=pl.BlockSpec((None, None, dv, tqt), lambda b, h, i: (b, h, 0, t0 + i)),
        input_output_aliases=aliases,
        scratch_shapes=[pltpu.VMEM((n_streams, 1, SUB), F32),
                        pltpu.VMEM((n_streams, dvx, SUB), F32),
                        pltpu.VMEM((2, n_streams, TK, SUB), F32),
                        pltpu.VMEM((2, n_streams, 1, SUB), F32),
                        pltpu.VMEM((2, n_streams, TK, SUB), BF16),
                        pltpu.VMEM((2, n_streams, 1, SUB), F32)]
        + ([pltpu.VMEM((2, n_streams, SUB, Kd), BF16)] if alibi else []),
        compiler_params=_cparams(("parallel", "parallel", "arbitrary")),
        name=("flash_alibi" if alibi else "flash_plain") + ("_meta" if meta_tile else ""),
    )(*args)


def flash_attention(q, kx, vt, *, heads, dv, feat0, sigma=None):
    N = q.shape[1]
    n = N - TAIL
    kw = dict(heads=heads, dv=dv, feat0=feat0)
    n_streams = N_STREAMS if sigma is None else N_STREAMS_ALIBI
    o = _flash_call(q, kx, vt, sigma, meta_tile=False, n_streams=n_streams, q0=0, n_q=n, **kw)
    o = _flash_call(q, kx, vt, sigma, meta_tile=True, n_streams=1, q0=n, n_q=SUB, prev=o, **kw)
    return o.reshape(o.shape[0], -1, N)


_ZQA, _ZQAR, _ZKA, _ZKAR, _ZVA = 0, 512, 1024, 1152, 1280
_ZQB, _ZKB, _ZVB = 1408, 1920, 2432
_ZCQ, _ZCKV, _ZKPE = 2944, 3200, 3328
DQK_C = C_NOPE + C_ROPE


def _rot_perm(d, block):
    j = np.arange(d)
    half = block // 2
    lower = (j % block) < half
    return np.where(lower, j + half, j - half), np.where(lower, -1.0, 1.0)


def _block_ones(n, width):
    g = np.arange(n) // width
    return jnp.asarray(g[:, None] == g[None, :], BF16)


def _placement(n_in, n_out, src_width, dst_stride, dst_off=0):
    p = np.zeros((n_in, n_out), np.float32)
    j = np.arange(n_in)
    p[j, (j // src_width) * dst_stride + dst_off + j % src_width] = 1.0
    return jnp.asarray(p, BF16)


def _group_sum(x2, ones):
    hi = x2.astype(BF16)
    lo = (x2 - hi.astype(F32)).astype(BF16)
    return (jnp.dot(hi, ones, preferred_element_type=F32)
            + jnp.dot(lo, ones, preferred_element_type=F32))


def _place(x_bf16, p):
    return jnp.dot(x_bf16, p, preferred_element_type=F32)


def _tile_lanes(x, reps):
    return jnp.concatenate([x] * reps, axis=1)


def _prep_kernel(x_ref, gx_ref, w_ref, tab_ref, qfb_ref, kfb_ref, gains_ref, gcq_ref, gckv_ref, wqb_ref, wkvb_ref,
                 s512_ref, s128_ref, s128x512_ref, e_pn_ref, e_np_ref, s64x256_ref,
                 pq_ref, pk_ref, pva_ref, ppe_ref,
                 qa_ref, kxa_ref, vta_ref, qb_ref, kxb_ref, vtb_ref, qc_ref, kxc_ref, vtc_ref,
                 *, n_valid):
    tm = x_ref.shape[0]
    x = x_ref[...]
    hn = (x * lax.rsqrt(jnp.mean(x * x, axis=-1, keepdims=True) + NORM_EPS) * gx_ref[...]).astype(BF16)

    def zcols(off, width):
        return jnp.dot(hn, w_ref[:, off:off + width], preferred_element_type=F32)

    rows = pl.program_id(1) * tm + lax.broadcasted_iota(jnp.int32, (tm, 1), 0)
    valid = (rows < n_valid).astype(F32)
    lane128 = lax.broadcasted_iota(jnp.int32, (1, 128), 1)
    flag_col = (lane128 == 64).astype(F32)
    one_col0 = (lane128 == 0).astype(F32)
    g = lambda i, w: gains_ref[i:i + 1, :w]
    cos_a, sin_a = tab_ref[:, 0:128], tab_ref[:, 128:256]
    cos_c, sin_c = tab_ref[:, 256:384], tab_ref[:, 384:512]
    cs_k = tab_ref[:, 512:640]
    rs = lambda ms: lax.rsqrt(ms + NORM_EPS)

    def v_transposed(v_ext, rows_out):
        return (v_ext * valid).T[:rows_out].astype(BF16)

    q, qr = zcols(_ZQA, 512), zcols(_ZQAR, 512)
    r = rs(_group_sum(q * q, s512_ref[...]) * (1.0 / A_HEAD_DIM))
    qo = ((q * g(0, 512) * _tile_lanes(cos_a, 4) + qr * g(1, 512) * _tile_lanes(sin_a, 4))
          * (r * (A_HEAD_DIM ** -0.5 * LOG2E)))
    mask_row = MASK_SCORE * _tile_lanes(flag_col, 8)
    qa_ref[...] = (_place(qo.astype(BF16), pq_ref[...]) + mask_row).astype(BF16)
    k, kr = zcols(_ZKA, 128), zcols(_ZKAR, 128)
    r = rs(_group_sum(k * k, s128_ref[...]) * (1.0 / A_HEAD_DIM))
    ko = (k * g(2, 128) * cos_a + kr * g(3, 128) * sin_a) * (r * valid)
    kxa_ref[...] = (_place(ko.astype(BF16), pk_ref[...])
                    + (1.0 - valid) * _tile_lanes(flag_col, 2)).astype(BF16)
    v = zcols(_ZVA, 128).astype(BF16)
    for h in range(A_KV_HEADS):
        v_ext = _place(v, pva_ref[h]) + flag_col
        vta_ref[h] = v_transposed(v_ext, A_HEAD_DIM + ONES_ROWS)

    q = zcols(_ZQB, 512)
    r = rs(_group_sum(q * q, s512_ref[...]) * (1.0 / B_HEAD_DIM))
    qo = q * g(4, 512) * (r * (B_HEAD_DIM ** -0.5))
    qb_ref[...] = (_place(qo.astype(BF16), pq_ref[...]) + qfb_ref[...]).astype(BF16)
    k = zcols(_ZKB, 512)
    r = rs(_group_sum(k * k, s512_ref[...]) * (1.0 / B_HEAD_DIM))
    ko = k * g(5, 512) * (r * valid)
    kxb_ref[...] = (_place(ko.astype(BF16), pq_ref[...]) + kfb_ref[...] * valid
                    + (1.0 - valid) * _tile_lanes((lane128 == 68).astype(F32), 8)).astype(BF16)
    for h in range(B_HEADS):
        v_h = zcols(_ZVB + h * B_V_DIM, B_V_DIM)
        v_ext = jnp.concatenate([v_h, jnp.broadcast_to(one_col0, (tm, 128))], axis=1)
        vtb_ref[h] = v_transposed(v_ext, B_V_DIM + ONES_ROWS)

    cq = zcols(_ZCQ, C_Q_LORA)
    cqn = cq * rs(jnp.mean(cq * cq, axis=-1, keepdims=True)) * gcq_ref[...]
    q2 = jnp.dot(cqn.astype(BF16), wqb_ref[...], preferred_element_type=F32)
    nope, pe, per = q2[:, :512], q2[:, 512:768], q2[:, 768:1024]
    nn, pp = nope * nope, pe * pe
    r_n = rs((_group_sum(nn, s128x512_ref[...]) + _group_sum(pp, e_pn_ref[...])) * (1.0 / DQK_C))
    r_p = rs((_group_sum(pp, s64x256_ref[...]) + _group_sum(nn, e_np_ref[...])) * (1.0 / DQK_C))
    sc = DQK_C ** -0.5 * LOG2E
    q_nope = (nope * g(6, 512) * (r_n * sc)).astype(BF16)
    q_pe = ((pe * g(7, 256) * _tile_lanes(cos_c, 2) + per * g(8, 256) * _tile_lanes(sin_c, 2))
            * (r_p * sc))
    pe_slots = (_place(q_pe.astype(BF16), ppe_ref[...])
                + MASK_SCORE * _tile_lanes(flag_col, 4)).astype(BF16)
    qc_ref[...] = jnp.concatenate(
        [t for h in range(C_HEADS)
         for t in (q_nope[:, 128 * h:128 * (h + 1)], pe_slots[:, 128 * h:128 * (h + 1)])], axis=1)

    ckv = zcols(_ZCKV, C_KV_LORA)
    ckvn = ckv * rs(jnp.mean(ckv * ckv, axis=-1, keepdims=True)) * gckv_ref[...]
    kv = jnp.dot(ckvn.astype(BF16), wkvb_ref[...], preferred_element_type=F32)
    k_nope = kv[:, :512]
    kk = zcols(_ZKPE, 128)
    pe_ss = jnp.sum(kk[:, :64] * kk[:, :64], axis=-1, keepdims=True)
    r_k = rs((_group_sum(k_nope * k_nope, s128x512_ref[...]) + pe_ss) * (1.0 / DQK_C))
    k_no = (k_nope * g(9, 512) * (r_k * valid)).astype(BF16)
    t = kk * g(10, 128) * cs_k
    k_pe = jnp.where(lane128 < 64, t + pltpu.roll(t, 64, axis=1), 0.0)
    pad_flag = ((1.0 - valid) * flag_col).astype(BF16)
    kxc_ref[...] = jnp.concatenate(
        [t2 for h in range(C_HEADS)
         for t2 in (k_no[:, 128 * h:128 * (h + 1)],
                    (k_pe * r_k[:, 128 * h:128 * h + 1] * valid).astype(BF16) + pad_flag)], axis=1)
    for h in range(C_HEADS):
        v_h = kv[:, 512 + h * C_V:512 + (h + 1) * C_V]
        v_ext = jnp.concatenate([v_h, jnp.broadcast_to(one_col0, (tm, 128))], axis=1)
        vtc_ref[h] = v_transposed(v_ext, C_V + ONES_ROWS)


def _positions(n):
    pad = TAIL - N_META
    rows = n // GRID_W
    row = np.concatenate([np.repeat(np.arange(rows), GRID_W), np.full(N_META, -1.0), np.zeros(pad)])
    col = np.concatenate([np.tile(np.arange(GRID_W), rows), np.arange(N_META), np.zeros(pad)])
    lin = np.concatenate([N_META + np.arange(n), np.arange(N_META), np.zeros(pad)])
    f = lambda a: jnp.asarray(a, F32)
    return f(row), f(col), f(lin)


def _rope_tables(n):
    row, col, lin = _positions(n)

    def angles(pos, d):
        inv = ROPE_THETA ** (-2.0 * jnp.arange(d // 2, dtype=F32) / d)
        ang = pos[:, None] * inv[None, :]
        return jnp.concatenate([ang, ang], axis=1)

    ang_a = jnp.concatenate([angles(row, A_HEAD_DIM // 2), angles(col, A_HEAD_DIM // 2)], axis=1)
    ang_c = angles(lin, C_ROPE)
    two = lambda t: jnp.concatenate([t, t], axis=1)
    return jnp.concatenate([two(jnp.cos(ang_a)), two(jnp.sin(ang_a)), two(jnp.cos(ang_c)),
                            two(jnp.sin(ang_c)), jnp.cos(ang_c), jnp.sin(ang_c)], axis=1)


def _alibi_reach(sigma, qk_bound):
    dist = (2.0 * QK_BOUND_SLACK * qk_bound - EXP_ZERO) / sigma
    return jnp.clip(jnp.floor(dist / TK), 0, 1 << 20).astype(jnp.int32)


def _alibi_tables():
    slopes = 2.0 ** (-8.0 * np.arange(1, B_HEADS + 1) / B_HEADS)
    sig = np.repeat(slopes, 2)
    idx = np.arange(ROW_TILE)
    ii, jj = idx % SUB, idx % TK
    qf = np.zeros((ROW_TILE, 2 * B_HEADS, 128), np.float32)
    kf = np.zeros((ROW_TILE, 2 * B_HEADS, 128), np.float32)
    ii_hi = (ii // 256) * 256
    qf[:, :, 64] = -ii_hi[:, None] * sig[None, :]
    qf[:, :, 65] = -(ii - ii_hi)[:, None] * sig[None, :]
    qf[:, :, 66] = 256.0 * sig[None, :]
    qf[:, :, 67] = sig[None, :]
    qf[:, :, 68] = MASK_SCORE
    kf[:, :, 64] = 1.0
    kf[:, :, 65] = 1.0
    kf[:, :, 66] = (jj // 256)[:, None]
    kf[:, :, 67] = (jj % 256)[:, None]
    flat = lambda a: jnp.asarray(a.reshape(ROW_TILE, -1), F32)
    return jnp.asarray(sig, F32), flat(qf), flat(kf)


def _prep_params(l, p):
    w = p['w_in'][l]
    cuts = np.concatenate([[0], np.cumsum(IN_SPLITS)])
    sec = lambda i: w[:, int(cuts[i]):int(cuts[i + 1])]
    qa, ka, va, qb, kb, vb, cq, ckv, kpe, gates = [sec(i) for i in range(10)]
    perm_a, sign_a = _rot_perm(A_HEAD_DIM, A_HEAD_DIM // 2)
    perm_c, sign_c = _rot_perm(C_ROPE, C_ROPE)

    def rot_cols(wm, perm, sign, heads):
        d = len(perm)
        cols = (np.arange(heads)[:, None] * d + perm[None, :]).reshape(-1)
        return wm[:, cols] * jnp.asarray(np.tile(sign, heads), F32)

    w_att = jnp.concatenate(
        [qa, rot_cols(qa, perm_a, sign_a, A_HEADS), ka, rot_cols(ka, perm_a, sign_a, A_KV_HEADS), va,
         qb, kb, vb, cq, ckv, kpe, rot_cols(kpe, perm_c, sign_c, 1)], axis=1).astype(BF16)
    w_gate = gates.astype(BF16)

    wq = p['c_w_q_b'][l].reshape(C_Q_LORA, C_HEADS, DQK_C)
    q_nope = wq[:, :, :C_NOPE].reshape(C_Q_LORA, -1)
    q_pe = wq[:, :, C_NOPE:].reshape(C_Q_LORA, -1)
    w_qb = jnp.concatenate([q_nope, q_pe, rot_cols(q_pe, perm_c, sign_c, C_HEADS)], 1).astype(BF16)
    wkv = p['c_w_kv_b'][l].reshape(C_KV_LORA, C_HEADS, C_NOPE + C_V)
    w_kvb = jnp.concatenate([wkv[:, :, :C_NOPE].reshape(C_KV_LORA, -1),
                             wkv[:, :, C_NOPE:].reshape(C_KV_LORA, -1)], 1).astype(BF16)

    tile = lambda v, r: jnp.tile(v, r)
    gaq, gak = p['a_q_norm_g'][l], p['a_k_norm_g'][l]
    gcq, gck = p['c_q_norm_g'][l], p['c_k_norm_g'][l]
    rows = [tile(gaq, 8), tile(gaq[perm_a], 8), tile(gak, 2), tile(gak[perm_a], 2),
            tile(p['b_q_norm_g'][l], 8), tile(p['b_k_norm_g'][l], 8),
            tile(gcq[:C_NOPE], 4), tile(gcq[C_NOPE:], 4), tile(gcq[C_NOPE:][perm_c], 4),
            tile(gck[:C_NOPE], 4), jnp.concatenate([gck[C_NOPE:], gck[C_NOPE:][perm_c]])]
    gains = jnp.stack([jnp.pad(r, (0, 512 - r.shape[0])) for r in rows]
                      + [jnp.zeros((512,), F32)] * 5)
    pva = jnp.stack([_placement(128, 128, 64, 0) * (np.arange(128)[:, None] // 64 == h)
                     for h in range(A_KV_HEADS)]).astype(BF16)
    consts = (_block_ones(512, 64), _block_ones(128, 64), _block_ones(512, 128),
              jnp.asarray(np.arange(256)[:, None] // 64 == np.arange(512)[None, :] // 128, BF16),
              jnp.asarray(np.arange(512)[:, None] // 128 == np.arange(256)[None, :] // 64, BF16),
              _block_ones(256, 64),
              _placement(512, 1024, 64, 128), _placement(128, 256, 64, 128), pva,
              _placement(256, 512, 64, 128))
    return w_gate, w_att, w_qb, w_kvb, gains, consts


def prep_attention_inputs(x, g_x, w_att, tabs, qfb, kfb, gains, g_cq, g_ckv, w_qb, w_kvb, consts, *,
                          n_valid):
    B, N, _ = x.shape
    tm = ROW_TILE
    row = lambda c: pl.BlockSpec((None, tm, c), lambda b, i: (b, i, 0))
    const = lambda a: pl.BlockSpec(a.shape, lambda b, i: (0,) * a.ndim)
    vt_spec = lambda g, d: pl.BlockSpec((None, g, d, tm), lambda b, i: (b, 0, 0, i))
    vt_shape = lambda g, d: jax.ShapeDtypeStruct((B, g, d, N), BF16)
    tok = lambda c: jax.ShapeDtypeStruct((B, N, c), BF16)
    small = [gains, g_cq.reshape(1, -1), g_ckv.reshape(1, -1), w_qb, w_kvb, *consts]
    return pl.pallas_call(
        functools.partial(_prep_kernel, n_valid=n_valid),
        out_shape=[tok(1024), tok(256), vt_shape(A_KV_HEADS, A_HEAD_DIM + ONES_ROWS),
                   tok(1024), tok(1024), vt_shape(B_HEADS, B_V_DIM + ONES_ROWS),
                   tok(1024), tok(1024), vt_shape(C_HEADS, C_V + ONES_ROWS)],
        grid=(B, N // tm),
        in_specs=[row(D_MODEL), const(g_x), const(w_att),
                  pl.BlockSpec((tm, tabs.shape[1]), lambda b, i: (i, 0)),
                  const(qfb), const(kfb)] + [const(a) for a in small],
        out_specs=[row(1024), row(256), vt_spec(A_KV_HEADS, A_HEAD_DIM + ONES_ROWS),
                   row(1024), row(1024), vt_spec(B_HEADS, B_V_DIM + ONES_ROWS),
                   row(1024), row(1024), vt_spec(C_HEADS, C_V + ONES_ROWS)],
        compiler_params=_cparams(("parallel", "parallel")),
        name="prep_attention_inputs",
    )(x, g_x, w_att, tabs, qfb, kfb, *small)


def _layer(x, tabs, l, p):
    B, N, _ = x.shape
    n_valid = N - TAIL + N_META
    w_gate, w_att, w_qb, w_kvb, gains, consts = _prep_params(l, p)
    g_x = p['attn_norm_g'][l].reshape(1, D_MODEL)
    sig, qfb, kfb = _alibi_tables()
    qa, kxa, vta, qb, kxb, vtb, qc, kxc, vtc = prep_attention_inputs(
        x, g_x, w_att, tabs, qfb, kfb, gains, p['c_q_a_norm_g'][l], p['c_kv_a_norm_g'][l], w_qb, w_kvb,
        consts, n_valid=n_valid)
    reach = _alibi_reach(sig, B_HEAD_DIM ** 0.5 * jnp.max(jnp.abs(p['b_q_norm_g'][l]))
                         * jnp.max(jnp.abs(p['b_k_norm_g'][l])))
    oa = flash_attention(qa, kxa, vta, heads=A_HEADS, dv=A_HEAD_DIM, feat0=A_HEAD_DIM)
    ob = flash_attention(qb, kxb, vtb, heads=2 * B_HEADS, dv=B_V_DIM, feat0=B_HEAD_DIM,
                         sigma=(sig, reach))
    oc = flash_attention(qc, kxc, vtc, heads=C_HEADS, dv=C_V, feat0=DQK_C)

    lam_init = 0.8 - 0.6 * math.exp(-0.3 * l)
    lam = (jnp.exp(jnp.sum(p['b_lambda_q1'][l] * p['b_lambda_k1'][l]))
           - jnp.exp(jnp.sum(p['b_lambda_q2'][l] * p['b_lambda_k2'][l])) + lam_init)
    x = merge_branches(lam, oa, ob, oc, g_x, w_gate, p['b_gate'][l], p['b_subln_g'][l], x,
                       p['w_branch_a'][l].astype(BF16), p['w_branch_b'][l].astype(BF16),
                       p['w_branch_c'][l].astype(BF16), p['w_out'][l].astype(BF16),
                       n_valid=n_valid, lam_init=lam_init)
    x2 = mlp_block(x.reshape(B * N, D_MODEL), p['mlp_norm_g'][l], p['w_up'][l].astype(BF16),
                   p['w_down'][l].astype(BF16))
    return x2.reshape(B, N, D_MODEL)


def _trunk(x, meta_tokens, p):
    B, n, _ = x.shape
    assert n % TK == 0 and n % (N_STREAMS * SUB) == 0 and n % GRID_W == 0
    tabs = _rope_tables(n)
    meta = jnp.broadcast_to(meta_tokens[None], (B, N_META, D_MODEL))
    h = jnp.concatenate([x, meta, jnp.zeros((B, TAIL - N_META, D_MODEL), F32)], axis=1)
    for l in range(DEPTH):
        h = _layer(h, tabs, l, p)
    return h[:, :n]


def kernel(x_prompt, x_sample, meta_tokens, attn_norm_g, w_in, b_gate, a_q_norm_g, a_k_norm_g, b_q_norm_g, b_k_norm_g, b_lambda_q1, b_lambda_k1, b_lambda_q2, b_lambda_k2, b_subln_g, c_q_a_norm_g, c_w_q_b, c_kv_a_norm_g, c_w_kv_b, c_q_norm_g, c_k_norm_g, w_branch_a, w_branch_b, w_branch_c, w_out, mlp_norm_g, w_up, w_down):
    p = dict(attn_norm_g=attn_norm_g, w_in=w_in, b_gate=b_gate,
             a_q_norm_g=a_q_norm_g, a_k_norm_g=a_k_norm_g,
             b_q_norm_g=b_q_norm_g, b_k_norm_g=b_k_norm_g,
             b_lambda_q1=b_lambda_q1, b_lambda_k1=b_lambda_k1,
             b_lambda_q2=b_lambda_q2, b_lambda_k2=b_lambda_k2, b_subln_g=b_subln_g,
             c_q_a_norm_g=c_q_a_norm_g, c_w_q_b=c_w_q_b,
             c_kv_a_norm_g=c_kv_a_norm_g, c_w_kv_b=c_w_kv_b,
             c_q_norm_g=c_q_norm_g, c_k_norm_g=c_k_norm_g,
             w_branch_a=w_branch_a, w_branch_b=w_branch_b, w_branch_c=w_branch_c,
             w_out=w_out, mlp_norm_g=mlp_norm_g, w_up=w_up, w_down=w_down)
    return (_trunk(x_prompt, meta_tokens, p), _trunk(x_sample, meta_tokens, p))
```

```python
import functools
import math

import numpy as np
import jax
import jax.numpy as jnp
from jax import lax
from jax.experimental import pallas as pl
from jax.experimental.pallas import tpu as pltpu

F32 = jnp.float32
BF16 = jnp.bfloat16

D_MODEL = 1024
DEPTH = 2
N_META = 16
GRID_W = 64
ROPE_THETA = 10000.0
NORM_EPS = 1e-6
A_HEADS, A_KV_HEADS, A_HEAD_DIM = 8, 2, 64
B_HEADS, B_HEAD_DIM = 4, 64
B_V_DIM = 2 * B_HEAD_DIM
C_HEADS, C_NOPE, C_ROPE, C_V = 4, 128, 64, 128
C_Q_LORA, C_KV_LORA = 256, 128
N_BRANCH = 3
D_FF = 4 * D_MODEL
IN_SPLITS = (
    A_HEADS * A_HEAD_DIM, A_KV_HEADS * A_HEAD_DIM, A_KV_HEADS * A_HEAD_DIM,
    2 * B_HEADS * B_HEAD_DIM, 2 * B_HEADS * B_HEAD_DIM, B_HEADS * B_V_DIM,
    C_Q_LORA, C_KV_LORA, C_ROPE,
    N_BRANCH * D_MODEL,
)
IN_COLS = sum(IN_SPLITS)

LOG2E = math.log2(math.e)
SUB = 256
N_STREAMS = 4
N_STREAMS_ALIBI = 2
TK = 512
TAIL = TK
ROW_TILE = 512
ONES_ROWS = 16
MASK_SCORE = -30000.0
EXP_ZERO = -110.0
QK_BOUND_SLACK = 1.02
VMEM_LIMIT = 52 * 1024 * 1024


def _cparams(sem):
    return pltpu.CompilerParams(dimension_semantics=sem, vmem_limit_bytes=VMEM_LIMIT)


def _dot_tn(a, w):
    return lax.dot_general(a.astype(BF16), w, (((0,), (0,)), ((), ())),
                           preferred_element_type=F32)


def _merge_kernel(lam_ref, oa_ref, ob_ref, oc_ref, gx_ref, wg_ref, bg_ref, gsub_ref, x_ref,
                  wa_ref, wb_ref, wc_ref, wo_ref, out_ref, *, n_valid, lam_init):
    pa = _dot_tn(oa_ref[...], wa_ref[...])
    heads = []
    for h in range(B_HEADS):
        o1 = ob_ref[(2 * h) * B_V_DIM:(2 * h + 1) * B_V_DIM, :]
        o2 = ob_ref[(2 * h + 1) * B_V_DIM:(2 * h + 2) * B_V_DIM, :]
        d = o1 - lam_ref[0] * o2
        r = lax.rsqrt(jnp.mean(d * d, axis=0, keepdims=True) + NORM_EPS)
        heads.append(d * r * gsub_ref[...] * (1.0 - lam_init))
    pb = _dot_tn(jnp.concatenate(heads, axis=0), wb_ref[...])
    pc = _dot_tn(oc_ref[...], wc_ref[...])
    x = x_ref[...]
    hn = (x * lax.rsqrt(jnp.mean(x * x, axis=-1, keepdims=True) + NORM_EPS) * gx_ref[...]).astype(BF16)
    g = jax.nn.sigmoid(jnp.dot(hn, wg_ref[...], preferred_element_type=F32) + bg_ref[...])
    merged = (g[:, :D_MODEL] * pa + g[:, D_MODEL:2 * D_MODEL] * pb
              + g[:, 2 * D_MODEL:] * pc)
    y = x + jnp.dot(merged.astype(BF16), wo_ref[...], preferred_element_type=F32)
    rows = pl.program_id(1) * ROW_TILE + lax.broadcasted_iota(jnp.int32, (ROW_TILE, 1), 0)
    out_ref[...] = jnp.where(rows < n_valid, y, 0.0)


def merge_branches(lam, oa, ob, oc, g_x, w_gate, b_gate, g_sub, x, wa, wb, wc, wo, *, n_valid,
                   lam_init):
    B, N, _ = x.shape
    col = lambda a: pl.BlockSpec((None, a.shape[1], ROW_TILE), lambda b, i: (b, 0, i))
    row = lambda c: pl.BlockSpec((None, ROW_TILE, c), lambda b, i: (b, i, 0))
    full = lambda a: pl.BlockSpec(a.shape, lambda b, i: (0, 0))
    return pl.pallas_call(
        functools.partial(_merge_kernel, n_valid=n_valid, lam_init=lam_init),
        out_shape=jax.ShapeDtypeStruct((B, N, D_MODEL), F32),
        grid=(B, N // ROW_TILE),
        in_specs=[pl.BlockSpec(memory_space=pltpu.SMEM), col(oa), col(ob), col(oc),
                  full(g_x), full(w_gate),
                  pl.BlockSpec((1, N_BRANCH * D_MODEL), lambda b, i: (0, 0)),
                  pl.BlockSpec((B_V_DIM, 1), lambda b, i: (0, 0)),
                  row(D_MODEL), full(wa), full(wb), full(wc), full(wo)],
        out_specs=row(D_MODEL),
        compiler_params=_cparams(("parallel", "parallel")),
        name="merge_branches",
    )(lam.reshape(1), oa, ob, oc, g_x, w_gate, b_gate.reshape(1, -1), g_sub.reshape(-1, 1), x,
      wa, wb, wc, wo)


def _mlp_kernel(x_ref, g_ref, wu_ref, wd_ref, out_ref, *, ff_chunk):
    x = x_ref[...]
    r = lax.rsqrt(jnp.mean(x * x, axis=-1, keepdims=True) + NORM_EPS)
    h = (x * r * g_ref[...]).astype(BF16)
    acc = x
    for c in range(D_FF // ff_chunk):
        u = jnp.dot(h, wu_ref[:, c * ff_chunk:(c + 1) * ff_chunk],
                    preferred_element_type=F32)
        a = jnp.square(jnp.maximum(u, 0.0)).astype(BF16)
        acc = acc + jnp.dot(a, wd_ref[c * ff_chunk:(c + 1) * ff_chunk, :],
                            preferred_element_type=F32)
    out_ref[...] = acc


def mlp_block(x, g, wu, wd):
    R = x.shape[0]
    return pl.pallas_call(
        functools.partial(_mlp_kernel, ff_chunk=1024),
        out_shape=jax.ShapeDtypeStruct((R, D_MODEL), F32),
        grid=(R // ROW_TILE,),
        in_specs=[
            pl.BlockSpec((ROW_TILE, D_MODEL), lambda i: (i, 0)),
            pl.BlockSpec((1, D_MODEL), lambda i: (0, 0)),
            pl.BlockSpec(wu.shape, lambda i: (0, 0)),
            pl.BlockSpec(wd.shape, lambda i: (0, 0)),
        ],
        out_specs=pl.BlockSpec((ROW_TILE, D_MODEL), lambda i: (i, 0)),
        compiler_params=_cparams(("parallel",)),
        name="mlp_block",
    )(x, g.reshape(1, -1), wu, wd)


def _dot_nt(a, b):
    return lax.dot_general(a, b, (((1,), (1,)), ((), ())), preferred_element_type=F32)


def _attn_kernel(*refs, n_chunks, dv, feat0, alibi, meta_tile, n_streams):
    if alibi:
        sigma_ref, reach_ref, q_ref, kx_ref, vt_ref = refs[:5]
    else:
        q_ref, kx_ref, vt_ref = refs[:3]
    n_scratch = 7 if alibi else 6
    o_ref = refs[-n_scratch - 1]
    m_ref, acc_ref, s_ring, bm_ring, p_ring, al_ring = refs[-n_scratch:][:6]
    q2_ref = refs[-1] if alibi else None
    tqt = n_streams * SUB
    qi = pl.program_id(2)
    streams = range(n_streams)

    m_ref[...] = jnp.full(m_ref.shape, -1e30, F32)
    acc_ref[...] = jnp.zeros(acc_ref.shape, F32)

    ex = jnp.exp if alibi else jnp.exp2

    def q_form(st, form):
        if alibi:
            return q2_ref[form, st]
        return q_ref[st * SUB:(st + 1) * SUB, :]

    if alibi:
        for st in streams:
            q = q_ref[st * SUB:(st + 1) * SUB, :]
            lane = lax.broadcasted_iota(jnp.int32, q.shape, 1)
            q2_ref[0, st] = q
            q2_ref[1, st] = jnp.where((lane >= feat0) & (lane < feat0 + 4), -q, q)
        sigma = sigma_ref[pl.program_id(1)]
        if meta_tile:
            n_diag, j_before = 1, jnp.int32(0)
        else:
            n_diag, j_before = max(1, tqt // TK), (qi * tqt) // TK
    else:
        n_diag = 0
    n_pure = (n_chunks if meta_tile else n_chunks + 1 - n_diag) if alibi else n_chunks + 1
    if alibi:
        reach = reach_ref[pl.program_id(1)]
        if meta_tile:
            lo, hi = jnp.int32(0), jnp.minimum(n_pure, reach + 1)
        else:
            lo = jnp.maximum(0, j_before - reach)
            hi = jnp.minimum(n_pure, j_before + reach + 2)
        odd = (hi - lo) % 2
        grow_lo = (odd == 1) & (lo > 0)
        lo = lo - jnp.where(grow_lo, 1, 0)
        hi = hi + jnp.where((odd == 1) & jnp.logical_not(grow_lo), 1, 0)
        cnt = hi - lo
        extras = [n_chunks] if meta_tile else [j_before + d for d in range(n_diag)]
    else:
        lo, cnt = 0, n_pure - n_pure % 2
        extras = [n_chunks] if n_pure % 2 else []
    assert n_pure % 2 == 0 or not alibi

    def base_i(st):
        return jnp.int32(0) if meta_tile else N_META + qi * tqt + st * SUB

    def delta(st, b):
        return (base_i(st) - jnp.where(b == n_chunks, 0, N_META + b * TK)).astype(F32)

    def chunk_rows(b):
        if isinstance(b, int):
            return pl.ds(b * TK, TK)
        return pl.ds(pl.multiple_of(b * TK, TK), TK)

    def block(pos, extra=None):
        if extra is not None:
            return extras[extra], None, alibi
        v = lo + pos
        if not alibi:
            return v, None, False
        if meta_tile:
            return v, True, False
        after = v > j_before
        b = jnp.where(v == 0, n_chunks, v - 1 + jnp.where(after, n_diag, 0))
        return b, after, False

    def stage_a(blk, slot):
        b, after, diag = blk
        kx = kx_ref[chunk_rows(b), :]
        for st in streams:
            if diag:
                c = sigma * delta(st, b)
                s = jnp.minimum(_dot_nt(kx, q_form(st, 0)) - c, _dot_nt(kx, q_form(st, 1)) + c)
            elif alibi:
                s = _dot_nt(kx, q_form(st, 1 if after is True else after.astype(jnp.int32)))
            else:
                s = _dot_nt(kx, q_form(st, 0))
            s_ring[slot, st] = s
            bm_ring[slot, st] = jnp.max(s, axis=0, keepdims=True)

    def stage_b(blk, slot):
        b, after, diag = blk
        for st in streams:
            shift = 0.0
            if alibi and not diag:
                d = sigma * delta(st, b)
                shift = jnp.where(after, d, -d)
            m_old = m_ref[st]
            m_new = jnp.maximum(m_old, bm_ring[slot, st] + shift)
            al_ring[slot, st] = ex(m_old - m_new)
            p_ring[slot, st] = ex(s_ring[slot, st] - (m_new - shift)).astype(BF16)
            m_ref[st] = m_new

    def stage_c(blk, slot):
        b, _, _ = blk
        vt = vt_ref[:, chunk_rows(b)]
        for st in streams:
            acc_ref[st] = acc_ref[st] * al_ring[slot, st] + jnp.dot(
                vt, p_ring[slot, st], preferred_element_type=F32)

    stage_a(block(0), 0)
    stage_a(block(1), 1)
    stage_b(block(0), 0)

    def pair(tp, carry):
        w = 2 * tp
        stage_a(block(w + 2), 0)
        stage_b(block(w + 1), 1)
        stage_c(block(w), 0)
        stage_a(block(w + 3), 1)
        stage_b(block(w + 2), 0)
        stage_c(block(w + 1), 1)
        return carry

    lax.fori_loop(0, cnt // 2 - 1, pair, 0)

    def at(k):
        return block(cnt + k) if k < 0 else block(None, extra=k)

    for k in range(len(extras) + 2):
        if k < len(extras):
            stage_a(at(k), k % 2)
        if k - 1 < len(extras):
            stage_b(at(k - 1), (k - 1) % 2)
        stage_c(at(k - 2), k % 2)

    for st in streams:
        acc = acc_ref[st]
        o_ref[:, st * SUB:(st + 1) * SUB] = acc[:dv, :] / acc[dv:dv + 1, :]


def _flash_call(q, kx, vt, sigma, *, heads, dv, feat0, meta_tile, n_streams, q0, n_q, prev=None):
    B, N, _ = q.shape
    H, Kd = heads, q.shape[2] // heads
    G, Gv, dvx = kx.shape[2] // Kd, vt.shape[1], vt.shape[2]
    n_chunks = (N - TAIL) // TK
    assert n_chunks * TK + TAIL == N
    tqt = n_streams * SUB
    assert n_q % tqt == 0 and q0 % tqt == 0
    t0 = q0 // tqt
    alibi = sigma is not None
    in_specs = [
        pl.BlockSpec((None, tqt, Kd), lambda b, h, i: (b, t0 + i, h)),
        pl.BlockSpec((None, N, Kd), lambda b, h, i: (b, 0, h // (H // G))),
        pl.BlockSpec((None, None, dvx, N), lambda b, h, i: (b, h // (H // Gv), 0, 0)),
    ]
    args = [q, kx, vt]
    if alibi:
        in_specs = [pl.BlockSpec(memory_space=pltpu.SMEM)] * 2 + in_specs
        args = list(sigma) + args
    aliases = {}
    if prev is not None:
        aliases = {len(args): 0}
        in_specs = in_specs + [pl.BlockSpec(memory_space=pl.ANY)]
        args = args + [prev]
    return pl.pallas_call(
        functools.partial(_attn_kernel, n_chunks=n_chunks, dv=dv, feat0=feat0, alibi=alibi,
                          meta_tile=meta_tile, n_streams=n_streams),
        out_shape=jax.ShapeDtypeStruct((B, H, dv, N), F32),
        grid=(B, H, n_q // tqt),
        in_specs=in_specs,
        out_specs=pl.BlockSpec((None, None, dv, tqt), lambda b, h, i: (b, h, 0, t0 + i)),
        input_output_aliases=aliases,
        scratch_shapes=[pltpu.VMEM((n_streams, 1, SUB), F32),
                        pltpu.VMEM((n_streams, dvx, SUB), F32),
                        pltpu.VMEM((2, n_streams, TK, SUB), F32),
                        pltpu.VMEM((2, n_streams, 1, SUB), F32),
                        pltpu.VMEM((2, n_streams, TK, SUB), BF16),
                        pltpu.VMEM((2, n_streams, 1, SUB), F32)]
        + ([pltpu.VMEM((2, n_streams, SUB, Kd), BF16)] if alibi else []),
        compiler_params=_cparams(("parallel", "parallel", "arbitrary")),
        name=("flash_alibi" if alibi else "flash_plain") + ("_meta" if meta_tile else ""),
    )(*args)


def flash_attention(q, kx, vt, *, heads, dv, feat0, sigma=None):
    N = q.shape[1]
    n = N - TAIL
    kw = dict(heads=heads, dv=dv, feat0=feat0)
    n_streams = N_STREAMS if sigma is None else N_STREAMS_ALIBI
    o = _flash_call(q, kx, vt, sigma, meta_tile=False, n_streams=n_streams, q0=0, n_q=n, **kw)
    o = _flash_call(q, kx, vt, sigma, meta_tile=True, n_streams=1, q0=n, n_q=SUB, prev=o, **kw)
    return o.reshape(o.shape[0], -1, N)


_ZQA, _ZQAR, _ZKA, _ZKAR, _ZVA = 0, 512, 1024, 1152, 1280
_ZQB, _ZKB, _ZVB = 1408, 1920, 2432
_ZCQ, _ZCKV, _ZKPE = 2944, 3200, 3328
DQK_C = C_NOPE + C_ROPE


def _rot_perm(d, block):
    j = np.arange(d)
    half = block // 2
    lower = (j % block) < half
    return np.where(lower, j + half, j - half), np.where(lower, -1.0, 1.0)


def _block_ones(n, width):
    g = np.arange(n) // width
    return jnp.asarray(g[:, None] == g[None, :], BF16)


def _placement(n_in, n_out, src_width, dst_stride, dst_off=0):
    p = np.zeros((n_in, n_out), np.float32)
    j = np.arange(n_in)
    p[j, (j // src_width) * dst_stride + dst_off + j % src_width] = 1.0
    return jnp.asarray(p, BF16)


def _group_sum(x2, ones):
    return jnp.dot(x2.astype(BF16), ones, preferred_element_type=F32)


def _place(x_bf16, p):
    return jnp.dot(x_bf16, p, preferred_element_type=F32)


def _tile_lanes(x, reps):
    return jnp.concatenate([x] * reps, axis=1)


def _prep_kernel(x_ref, gx_ref, w_ref, tab_ref, qfb_ref, kfb_ref, gains_ref, gcq_ref, gckv_ref, wqb_ref, wkvb_ref,
                 s512_ref, s128_ref, s128x512_ref, e_pn_ref, e_np_ref, s64x256_ref,
                 pq_ref, pk_ref, pva_ref, ppe_ref,
                 qa_ref, kxa_ref, vta_ref, qb_ref, kxb_ref, vtb_ref, qc_ref, kxc_ref, vtc_ref,
                 *, n_valid):
    tm = x_ref.shape[0]
    x = x_ref[...]
    hn = (x * lax.rsqrt(jnp.mean(x * x, axis=-1, keepdims=True) + NORM_EPS) * gx_ref[...]).astype(BF16)

    def zcols(off, width):
        return jnp.dot(hn, w_ref[:, off:off + width], preferred_element_type=F32)

    rows = pl.program_id(1) * tm + lax.broadcasted_iota(jnp.int32, (tm, 1), 0)
    valid = (rows < n_valid).astype(F32)
    lane128 = lax.broadcasted_iota(jnp.int32, (1, 128), 1)
    flag_col = (lane128 == 64).astype(F32)
    one_col0 = (lane128 == 0).astype(F32)
    g = lambda i, w: gains_ref[i:i + 1, :w]
    cos_a, sin_a = tab_ref[:, 0:128], tab_ref[:, 128:256]
    cos_c, sin_c = tab_ref[:, 256:384], tab_ref[:, 384:512]
    cs_k = tab_ref[:, 512:640]
    rs = lambda ms: lax.rsqrt(ms + NORM_EPS)

    def v_transposed(v_ext, rows_out):
        return (v_ext * valid).T[:rows_out].astype(BF16)

    q, qr = zcols(_ZQA, 512), zcols(_ZQAR, 512)
    r = rs(_group_sum(q * q, s512_ref[...]) * (1.0 / A_HEAD_DIM))
    qo = ((q * g(0, 512) * _tile_lanes(cos_a, 4) + qr * g(1, 512) * _tile_lanes(sin_a, 4))
          * (r * (A_HEAD_DIM ** -0.5 * LOG2E)))
    mask_row = MASK_SCORE * _tile_lanes(flag_col, 8)
    qa_ref[...] = (_place(qo.astype(BF16), pq_ref[...]) + mask_row).astype(BF16)
    k, kr = zcols(_ZKA, 128), zcols(_ZKAR, 128)
    r = rs(_group_sum(k * k, s128_ref[...]) * (1.0 / A_HEAD_DIM))
    ko = (k * g(2, 128) * cos_a + kr * g(3, 128) * sin_a) * (r * valid)
    kxa_ref[...] = (_place(ko.astype(BF16), pk_ref[...])
                    + (1.0 - valid) * _tile_lanes(flag_col, 2)).astype(BF16)
    v = zcols(_ZVA, 128).astype(BF16)
    for h in range(A_KV_HEADS):
        v_ext = _place(v, pva_ref[h]) + flag_col
        vta_ref[h] = v_transposed(v_ext, A_HEAD_DIM + ONES_ROWS)

    q = zcols(_ZQB, 512)
    r = rs(_group_sum(q * q, s512_ref[...]) * (1.0 / B_HEAD_DIM))
    qo = q * g(4, 512) * (r * (B_HEAD_DIM ** -0.5))
    qb_ref[...] = (_place(qo.astype(BF16), pq_ref[...]) + qfb_ref[...]).astype(BF16)
    k = zcols(_ZKB, 512)
    r = rs(_group_sum(k * k, s512_ref[...]) * (1.0 / B_HEAD_DIM))
    ko = k * g(5, 512) * (r * valid)
    kxb_ref[...] = (_place(ko.astype(BF16), pq_ref[...]) + kfb_ref[...] * valid
                    + (1.0 - valid) * _tile_lanes((lane128 == 68).astype(F32), 8)).astype(BF16)
    for h in range(B_HEADS):
        v_h = zcols(_ZVB + h * B_V_DIM, B_V_DIM)
        v_ext = jnp.concatenate([v_h, jnp.broadcast_to(one_col0, (tm, 128))], axis=1)
        vtb_ref[h] = v_transposed(v_ext, B_V_DIM + ONES_ROWS)

    cq = zcols(_ZCQ, C_Q_LORA)
    cqn = cq * rs(jnp.mean(cq * cq, axis=-1, keepdims=True)) * gcq_ref[...]
    q2 = jnp.dot(cqn.astype(BF16), wqb_ref[...], preferred_element_type=F32)
    nope, pe, per = q2[:, :512], q2[:, 512:768], q2[:, 768:1024]
    nn, pp = nope * nope, pe * pe
    r_n = rs((_group_sum(nn, s128x512_ref[...]) + _group_sum(pp, e_pn_ref[...])) * (1.0 / DQK_C))
    r_p = rs((_group_sum(pp, s64x256_ref[...]) + _group_sum(nn, e_np_ref[...])) * (1.0 / DQK_C))
    sc = DQK_C ** -0.5 * LOG2E
    q_nope = (nope * g(6, 512) * (r_n * sc)).astype(BF16)
    q_pe = ((pe * g(7, 256) * _tile_lanes(cos_c, 2) + per * g(8, 256) * _tile_lanes(sin_c, 2))
            * (r_p * sc))
    pe_slots = (_place(q_pe.astype(BF16), ppe_ref[...])
                + MASK_SCORE * _tile_lanes(flag_col, 4)).astype(BF16)
    qc_ref[...] = jnp.concatenate(
        [t for h in range(C_HEADS)
         for t in (q_nope[:, 128 * h:128 * (h + 1)], pe_slots[:, 128 * h:128 * (h + 1)])], axis=1)

    ckv = zcols(_ZCKV, C_KV_LORA)
    ckvn = ckv * rs(jnp.mean(ckv * ckv, axis=-1, keepdims=True)) * gckv_ref[...]
    kv = jnp.dot(ckvn.astype(BF16), wkvb_ref[...], preferred_element_type=F32)
    k_nope = kv[:, :512]
    kk = zcols(_ZKPE, 128)
    pe_ss = jnp.sum(kk[:, :64] * kk[:, :64], axis=-1, keepdims=True)
    r_k = rs((_group_sum(k_nope * k_nope, s128x512_ref[...]) + pe_ss) * (1.0 / DQK_C))
    k_no = (k_nope * g(9, 512) * (r_k * valid)).astype(BF16)
    t = kk * g(10, 128) * cs_k
    k_pe = jnp.where(lane128 < 64, t + pltpu.roll(t, 64, axis=1), 0.0)
    pad_flag = ((1.0 - valid) * flag_col).astype(BF16)
    kxc_ref[...] = jnp.concatenate(
        [t2 for h in range(C_HEADS)
         for t2 in (k_no[:, 128 * h:128 * (h + 1)],
                    (k_pe * r_k[:, 128 * h:128 * h + 1] * valid).astype(BF16) + pad_flag)], axis=1)
    for h in range(C_HEADS):
        v_h = kv[:, 512 + h * C_V:512 + (h + 1) * C_V]
        v_ext = jnp.concatenate([v_h, jnp.broadcast_to(one_col0, (tm, 128))], axis=1)
        vtc_ref[h] = v_transposed(v_ext, C_V + ONES_ROWS)


def _positions(n):
    pad = TAIL - N_META
    rows = n // GRID_W
    row = np.concatenate([np.repeat(np.arange(rows), GRID_W), np.full(N_META, -1.0), np.zeros(pad)])
    col = np.concatenate([np.tile(np.arange(GRID_W), rows), np.arange(N_META), np.zeros(pad)])
    lin = np.concatenate([N_META + np.arange(n), np.arange(N_META), np.zeros(pad)])
    f = lambda a: jnp.asarray(a, F32)
    return f(row), f(col), f(lin)


def _rope_tables(n):
    row, col, lin = _positions(n)

    def angles(pos, d):
        inv = ROPE_THETA ** (-2.0 * jnp.arange(d // 2, dtype=F32) / d)
        ang = pos[:, None] * inv[None, :]
        return jnp.concatenate([ang, ang], axis=1)

    ang_a = jnp.concatenate([angles(row, A_HEAD_DIM // 2), angles(col, A_HEAD_DIM // 2)], axis=1)
    ang_c = angles(lin, C_ROPE)
    two = lambda t: jnp.concatenate([t, t], axis=1)
    return jnp.concatenate([two(jnp.cos(ang_a)), two(jnp.sin(ang_a)), two(jnp.cos(ang_c)),
                            two(jnp.sin(ang_c)), jnp.cos(ang_c), jnp.sin(ang_c)], axis=1)


def _alibi_reach(sigma, qk_bound):
    dist = (2.0 * QK_BOUND_SLACK * qk_bound - EXP_ZERO) / sigma
    return jnp.clip(jnp.floor(dist / TK), 0, 1 << 20).astype(jnp.int32)


def _alibi_tables():
    slopes = 2.0 ** (-8.0 * np.arange(1, B_HEADS + 1) / B_HEADS)
    sig = np.repeat(slopes, 2)
    idx = np.arange(ROW_TILE)
    ii, jj = idx % SUB, idx % TK
    qf = np.zeros((ROW_TILE, 2 * B_HEADS, 128), np.float32)
    kf = np.zeros((ROW_TILE, 2 * B_HEADS, 128), np.float32)
    ii_hi = (ii // 256) * 256
    qf[:, :, 64] = -ii_hi[:, None] * sig[None, :]
    qf[:, :, 65] = -(ii - ii_hi)[:, None] * sig[None, :]
    qf[:, :, 66] = 256.0 * sig[None, :]
    qf[:, :, 67] = sig[None, :]
    qf[:, :, 68] = MASK_SCORE
    kf[:, :, 64] = 1.0
    kf[:, :, 65] = 1.0
    kf[:, :, 66] = (jj // 256)[:, None]
    kf[:, :, 67] = (jj % 256)[:, None]
    flat = lambda a: jnp.asarray(a.reshape(ROW_TILE, -1), F32)
    return jnp.asarray(sig, F32), flat(qf), flat(kf)


def _prep_params(l, p):
    w = p['w_in'][l]
    cuts = np.concatenate([[0], np.cumsum(IN_SPLITS)])
    sec = lambda i: w[:, int(cuts[i]):int(cuts[i + 1])]
    qa, ka, va, qb, kb, vb, cq, ckv, kpe, gates = [sec(i) for i in range(10)]
    perm_a, sign_a = _rot_perm(A_HEAD_DIM, A_HEAD_DIM // 2)
    perm_c, sign_c = _rot_perm(C_ROPE, C_ROPE)

    def rot_cols(wm, perm, sign, heads):
        d = len(perm)
        cols = (np.arange(heads)[:, None] * d + perm[None, :]).reshape(-1)
        return wm[:, cols] * jnp.asarray(np.tile(sign, heads), F32)

    w_att = jnp.concatenate(
        [qa, rot_cols(qa, perm_a, sign_a, A_HEADS), ka, rot_cols(ka, perm_a, sign_a, A_KV_HEADS), va,
         qb, kb, vb, cq, ckv, kpe, rot_cols(kpe, perm_c, sign_c, 1)], axis=1).astype(BF16)
    w_gate = gates.astype(BF16)

    wq = p['c_w_q_b'][l].reshape(C_Q_LORA, C_HEADS, DQK_C)
    q_nope = wq[:, :, :C_NOPE].reshape(C_Q_LORA, -1)
    q_pe = wq[:, :, C_NOPE:].reshape(C_Q_LORA, -1)
    w_qb = jnp.concatenate([q_nope, q_pe, rot_cols(q_pe, perm_c, sign_c, C_HEADS)], 1).astype(BF16)
    wkv = p['c_w_kv_b'][l].reshape(C_KV_LORA, C_HEADS, C_NOPE + C_V)
    w_kvb = jnp.concatenate([wkv[:, :, :C_NOPE].reshape(C_KV_LORA, -1),
                             wkv[:, :, C_NOPE:].reshape(C_KV_LORA, -1)], 1).astype(BF16)

    tile = lambda v, r: jnp.tile(v, r)
    gaq, gak = p['a_q_norm_g'][l], p['a_k_norm_g'][l]
    gcq, gck = p['c_q_norm_g'][l], p['c_k_norm_g'][l]
    rows = [tile(gaq, 8), tile(gaq[perm_a], 8), tile(gak, 2), tile(gak[perm_a], 2),
            tile(p['b_q_norm_g'][l], 8), tile(p['b_k_norm_g'][l], 8),
            tile(gcq[:C_NOPE], 4), tile(gcq[C_NOPE:], 4), tile(gcq[C_NOPE:][perm_c], 4),
            tile(gck[:C_NOPE], 4), jnp.concatenate([gck[C_NOPE:], gck[C_NOPE:][perm_c]])]
    gains = jnp.stack([jnp.pad(r, (0, 512 - r.shape[0])) for r in rows]
                      + [jnp.zeros((512,), F32)] * 5)
    pva = jnp.stack([_placement(128, 128, 64, 0) * (np.arange(128)[:, None] // 64 == h)
                     for h in range(A_KV_HEADS)]).astype(BF16)
    consts = (_block_ones(512, 64), _block_ones(128, 64), _block_ones(512, 128),
              jnp.asarray(np.arange(256)[:, None] // 64 == np.arange(512)[None, :] // 128, BF16),
              jnp.asarray(np.arange(512)[:, None] // 128 == np.arange(256)[None, :] // 64, BF16),
              _block_ones(256, 64),
              _placement(512, 1024, 64, 128), _placement(128, 256, 64, 128), pva,
              _placement(256, 512, 64, 128))
    return w_gate, w_att, w_qb, w_kvb, gains, consts


def prep_attention_inputs(x, g_x, w_att, tabs, qfb, kfb, gains, g_cq, g_ckv, w_qb, w_kvb, consts, *,
                          n_valid):
    B, N, _ = x.shape
    tm = ROW_TILE
    row = lambda c: pl.BlockSpec((None, tm, c), lambda b, i: (b, i, 0))
    const = lambda a: pl.BlockSpec(a.shape, lambda b, i: (0,) * a.ndim)
    vt_spec = lambda g, d: pl.BlockSpec((None, g, d, tm), lambda b, i: (b, 0, 0, i))
    vt_shape = lambda g, d: jax.ShapeDtypeStruct((B, g, d, N), BF16)
    tok = lambda c: jax.ShapeDtypeStruct((B, N, c), BF16)
    small = [gains, g_cq.reshape(1, -1), g_ckv.reshape(1, -1), w_qb, w_kvb, *consts]
    return pl.pallas_call(
        functools.partial(_prep_kernel, n_valid=n_valid),
        out_shape=[tok(1024), tok(256), vt_shape(A_KV_HEADS, A_HEAD_DIM + ONES_ROWS),
                   tok(1024), tok(1024), vt_shape(B_HEADS, B_V_DIM + ONES_ROWS),
                   tok(1024), tok(1024), vt_shape(C_HEADS, C_V + ONES_ROWS)],
        grid=(B, N // tm),
        in_specs=[row(D_MODEL), const(g_x), const(w_att),
                  pl.BlockSpec((tm, tabs.shape[1]), lambda b, i: (i, 0)),
                  const(qfb), const(kfb)] + [const(a) for a in small],
        out_specs=[row(1024), row(256), vt_spec(A_KV_HEADS, A_HEAD_DIM + ONES_ROWS),
                   row(1024), row(1024), vt_spec(B_HEADS, B_V_DIM + ONES_ROWS),
                   row(1024), row(1024), vt_spec(C_HEADS, C_V + ONES_ROWS)],
        compiler_params=_cparams(("parallel", "parallel")),
        name="prep_attention_inputs",
    )(x, g_x, w_att, tabs, qfb, kfb, *small)


def _layer(x, tabs, l, p):
    B, N, _ = x.shape
    n_valid = N - TAIL + N_META
    w_gate, w_att, w_qb, w_kvb, gains, consts = _prep_params(l, p)
    g_x = p['attn_norm_g'][l].reshape(1, D_MODEL)
    sig, qfb, kfb = _alibi_tables()
    qa, kxa, vta, qb, kxb, vtb, qc, kxc, vtc = prep_attention_inputs(
        x, g_x, w_att, tabs, qfb, kfb, gains, p['c_q_a_norm_g'][l], p['c_kv_a_norm_g'][l], w_qb, w_kvb,
        consts, n_valid=n_valid)
    reach = _alibi_reach(sig, B_HEAD_DIM ** 0.5 * jnp.max(jnp.abs(p['b_q_norm_g'][l]))
                         * jnp.max(jnp.abs(p['b_k_norm_g'][l])))
    oa = flash_attention(qa, kxa, vta, heads=A_HEADS, dv=A_HEAD_DIM, feat0=A_HEAD_DIM)
    ob = flash_attention(qb, kxb, vtb, heads=2 * B_HEADS, dv=B_V_DIM, feat0=B_HEAD_DIM,
                         sigma=(sig, reach))
    oc = flash_attention(qc, kxc, vtc, heads=C_HEADS, dv=C_V, feat0=DQK_C)

    lam_init = 0.8 - 0.6 * math.exp(-0.3 * l)
    lam = (jnp.exp(jnp.sum(p['b_lambda_q1'][l] * p['b_lambda_k1'][l]))
           - jnp.exp(jnp.sum(p['b_lambda_q2'][l] * p['b_lambda_k2'][l])) + lam_init)
    x = merge_branches(lam, oa, ob, oc, g_x, w_gate, p['b_gate'][l], p['b_subln_g'][l], x,
                       p['w_branch_a'][l].astype(BF16), p['w_branch_b'][l].astype(BF16),
                       p['w_branch_c'][l].astype(BF16), p['w_out'][l].astype(BF16),
                       n_valid=n_valid, lam_init=lam_init)
    x2 = mlp_block(x.reshape(B * N, D_MODEL), p['mlp_norm_g'][l], p['w_up'][l].astype(BF16),
                   p['w_down'][l].astype(BF16))
    return x2.reshape(B, N, D_MODEL)


def _trunk(x, meta_tokens, p):
    B, n, _ = x.shape
    assert n % TK == 0 and n % (N_STREAMS * SUB) == 0 and n % GRID_W == 0
    tabs = _rope_tables(n)
    meta = jnp.broadcast_to(meta_tokens[None], (B, N_META, D_MODEL))
    h = jnp.concatenate([x, meta, jnp.zeros((B, TAIL - N_META, D_MODEL), F32)], axis=1)
    for l in range(DEPTH):
        h = _layer(h, tabs, l, p)
    return h[:, :n]


def kernel(x_prompt, x_sample, meta_tokens, attn_norm_g, w_in, b_gate, a_q_norm_g, a_k_norm_g, b_q_norm_g, b_k_norm_g, b_lambda_q1, b_lambda_k1, b_lambda_q2, b_lambda_k2, b_subln_g, c_q_a_norm_g, c_w_q_b, c_kv_a_norm_g, c_w_kv_b, c_q_norm_g, c_k_norm_g, w_branch_a, w_branch_b, w_branch_c, w_out, mlp_norm_g, w_up, w_down):
    p = dict(attn_norm_g=attn_norm_g, w_in=w_in, b_gate=b_gate,
             a_q_norm_g=a_q_norm_g, a_k_norm_g=a_k_norm_g,
             b_q_norm_g=b_q_norm_g, b_k_norm_g=b_k_norm_g,
             b_lambda_q1=b_lambda_q1, b_lambda_k1=b_lambda_k1,
             b_lambda_q2=b_lambda_q2, b_lambda_k2=b_lambda_k2, b_subln_g=b_subln_g,
             c_q_a_norm_g=c_q_a_norm_g, c_w_q_b=c_w_q_b,
             c_kv_a_norm_g=c_kv_a_norm_g, c_w_kv_b=c_w_kv_b,
             c_q_norm_g=c_q_norm_g, c_k_norm_g=c_k_norm_g,
             w_branch_a=w_branch_a, w_branch_b=w_branch_b, w_branch_c=w_branch_c,
             w_out=w_out, mlp_norm_g=mlp_norm_g, w_up=w_up, w_down=w_down)
    return (_trunk(x_prompt, meta_tokens, p), _trunk(x_sample, meta_tokens, p))
```
